```python
import functools
import jax, jax.numpy as jnp
from jax import lax
import numpy as np

D_MODEL = 1024
BATCH = 2
SEQ = 8192
DEPTH = 2
DEC_BATCH = 32
DEC_SEQ = 4
PAST_LEN = 16384
PAGE_SIZE = 128

M_HEADS = 4
M_HEAD_DIM = D_MODEL // 8
D_A = M_HEADS * M_HEAD_DIM
CONV_W = 4
M_CHUNK = 128
N_HEADS = 8
HEAD_DIM = D_MODEL // 16
D_B = N_HEADS * HEAD_DIM
KV_GROUPS = 2
HEADS_PER_GROUP = N_HEADS // KV_GROUPS
KV_W = KV_GROUPS * HEAD_DIM
CMP_STRIDE = 16
CMP_BLOCK = 2 * CMP_STRIDE
SLC_BLOCK = 64
N_SELECT = 16
WINDOW = 512
Q_BLOCK = 128
D_FF = -(-8 * D_MODEL // (3 * 256)) * 256
RMS_EPS = 1e-6
NEG = -1e30
BIG = 1e30
SIZES = (2 * D_A, D_A, D_A, M_HEADS, M_HEADS, D_B, KV_W, KV_W, KV_W, KV_W, KV_W, KV_W, 3 * N_HEADS, 2 * D_MODEL)
SPLITS = tuple(int(s) for s in np.cumsum(SIZES)[:-1])
D_IN = sum(SIZES)

kernel_name = "hybrid_mlstm_nsa_decoder_step"


def rmsnorm(x, g):
    xf = x.astype(jnp.float32)
    y = xf * lax.rsqrt(jnp.mean(xf * xf, axis=-1, keepdims=True) + RMS_EPS)
    return (y * g.astype(jnp.float32)).astype(x.dtype)


def alibi_slopes():
    return jnp.asarray(2.0 ** (-8.0 * np.arange(1, N_HEADS + 1) / N_HEADS), jnp.float32)


def masked_softmax(s, mask):
    p = jax.nn.softmax(jnp.where(mask, s, NEG), axis=-1)
    return jnp.where(mask, p, 0.0)


def pad_rows(a, mult):
    pad = (-a.shape[1]) % mult
    return jnp.pad(a, ((0, 0), (0, pad)) + ((0, 0),) * (a.ndim - 2))


def gather_pages(pool, page_table):
    rows = pool[page_table]
    return rows.reshape((page_table.shape[0], -1) + pool.shape[2:])


def causal_conv(u, buf, w, b):
    T = u.shape[1]
    ext = jnp.concatenate([buf.astype(u.dtype), u], axis=1)
    out = b + ext[:, 0:T] * w[0]
    for j in range(1, CONV_W):
        out = out + ext[:, j:j + T] * w[j]
    return out, ext[:, T:]


def mlstm_chunk(carry, inp):
    C, n, m = carry
    q, k, v, ig, lf = inp
    L = q.shape[1]
    Fh = jnp.cumsum(lf, axis=1).transpose(0, 2, 1)
    igh = ig.transpose(0, 2, 1)
    causal = jnp.tril(jnp.ones((L, L), dtype=bool))
    D = jnp.where(causal, Fh[:, :, :, None] - Fh[:, :, None, :] + igh[:, :, None, :], NEG)
    a = Fh + m[:, :, None]
    m_t = jnp.maximum(a, D.max(-1))
    S = jnp.einsum('bthd,bshd->bhts', q, k) * jnp.exp(D - m_t[..., None])
    inter = jnp.exp(a - m_t)
    num = jnp.einsum('bhts,bshd->bthd', S, v) + inter.transpose(0, 2, 1)[..., None] * jnp.einsum('bthk,bhkv->bthv', q, C)
    den = S.sum(-1) + inter * jnp.einsum('bthk,bhk->bht', q, n)
    denom = jnp.maximum(jnp.abs(den), jnp.exp(-m_t)).transpose(0, 2, 1)[..., None]
    h = num / denom
    F_L = Fh[:, :, -1]
    w_s = F_L[:, :, None] - Fh + igh
    m_new = jnp.maximum(F_L + m, w_s.max(-1))
    decay = jnp.exp(F_L + m - m_new)
    ws = jnp.exp(w_s - m_new[..., None])
    C_new = decay[..., None, None] * C + jnp.einsum('bhs,bshk,bshv->bhkv', ws, k, v)
    n_new = decay[..., None] * n + jnp.einsum('bhs,bshk->bhk', ws, k)
    return (C_new, n_new, m_new), h


def mlstm_branch(qk_pre, v_pre, o_pre, i_pre, f_pre, conv_buf, C, n, m, conv_w, conv_b, head_g):
    f32 = jnp.float32
    B, T = qk_pre.shape[:2]
    qk, conv_new = causal_conv(qk_pre, conv_buf, conv_w, conv_b)
    qk = jax.nn.silu(qk.astype(f32))
    q = qk[..., :D_A].reshape(B, T, M_HEADS, M_HEAD_DIM)
    k = qk[..., D_A:].reshape(B, T, M_HEADS, M_HEAD_DIM) * (M_HEAD_DIM ** -0.5)
    v = v_pre.astype(f32).reshape(B, T, M_HEADS, M_HEAD_DIM)
    ig = i_pre.astype(f32)
    lf = jax.nn.log_sigmoid(f_pre.astype(f32))
    L = M_CHUNK if T % M_CHUNK == 0 else T
    nc = T // L

    def chunks(a):
        return jnp.swapaxes(a.reshape((B, nc, L) + a.shape[2:]), 0, 1)

    (C1, n1, m1), h = lax.scan(mlstm_chunk, (C, n, m), (chunks(q), chunks(k), chunks(v), chunks(ig), chunks(lf)))
    h = jnp.swapaxes(h, 0, 1).reshape(B, T, M_HEADS, M_HEAD_DIM)
    mu = jnp.mean(h, axis=-1, keepdims=True)
    var = jnp.mean(jnp.square(h - mu), axis=-1, keepdims=True)
    hn = ((h - mu) * lax.rsqrt(var + RMS_EPS)).reshape(B, T, D_A) * head_g.astype(f32)
    out = jax.nn.sigmoid(o_pre.astype(f32)) * hn
    return out.astype(qk_pre.dtype), conv_new, C1, n1, m1


def compress(rows, w_pos, w_proj):
    B, Lp = rows.shape[:2]
    sub = rows.reshape(B, Lp // CMP_STRIDE, CMP_STRIDE, KV_GROUPS, HEAD_DIM)
    head = jnp.einsum('bnjgd,jd->bngd', sub, w_pos[:CMP_STRIDE])
    tail = jnp.einsum('bnjgd,jd->bngd', sub, w_pos[CMP_STRIDE:])
    pooled = head[:, :-1] + tail[:, 1:]
    return jnp.einsum('bngd,de->bnge', pooled, w_proj)


def to_blocks(a):
    B, Lp = a.shape[:2]
    return a.reshape(B, Lp // SLC_BLOCK, SLC_BLOCK, KV_GROUPS, HEAD_DIM).transpose(0, 3, 1, 2, 4)


def nsa_attend(q, qpos, g, kc, vc, c_end, ksb, vsb, kw, vw, wpos, slopes):
    f32 = jnp.float32
    B, T = q.shape[:2]
    G, I, dh = KV_GROUPS, HEADS_PER_GROUP, HEAD_DIM
    scale = dh ** -0.5
    qg = q.reshape(B, T, G, I, dh)
    gg = g.reshape(B, T, G, I, 3)
    sl = slopes.reshape(G, I)[None, None, :, :, None]
    tq = qpos.astype(f32)
    dist_c = tq[:, None] - c_end[None, :].astype(f32)
    mask_c = (c_end[None, :] <= qpos[:, None])[None, :, None, None, :]
    s_c = jnp.einsum('btgid,bngd->btgin', qg, kc).astype(f32) * scale - sl * dist_c[None, :, None, None, :]
    p_c = masked_softmax(s_c, mask_c)
    o_c = jnp.einsum('btgin,bngd->btgid', p_c.astype(vc.dtype), vc)
    NC = kc.shape[1]
    NS = ksb.shape[2]
    c_start = jnp.arange(NC, dtype=jnp.int32) * CMP_STRIDE
    s_start = jnp.arange(NS, dtype=jnp.int32) * SLC_BLOCK
    overlap = ((c_start[:, None] < s_start[None, :] + SLC_BLOCK) & (c_start[:, None] + CMP_BLOCK > s_start[None, :])).astype(f32)
    imp = jnp.einsum('btgin,nj->btgj', p_c, overlap)
    blk = jnp.arange(NS, dtype=jnp.int32)[None, :]
    cur = (qpos // SLC_BLOCK)[:, None]
    forced = (blk == 0) | (blk == cur) | (blk == cur - 1)
    valid = s_start[None, :] <= qpos[:, None]
    score = jnp.where(forced[None, :, None, :], BIG, jnp.where(valid[None, :, None, :], imp, NEG))
    n_sel = min(N_SELECT, NS)
    _, idx = lax.top_k(score, n_sel)
    idx_g = idx.transpose(0, 2, 1, 3).reshape(B, G, T * n_sel)
    gather = jax.vmap(jax.vmap(lambda a, i: a[i]))
    kg = gather(ksb, idx_g).reshape(B, G, T, n_sel, SLC_BLOCK, dh)
    vg = gather(vsb, idx_g).reshape(B, G, T, n_sel, SLC_BLOCK, dh)
    pos_s = idx[..., None] * SLC_BLOCK + jnp.arange(SLC_BLOCK, dtype=jnp.int32)
    dist_s = tq[None, :, None, None, None] - pos_s.astype(f32)
    mask_s = (pos_s <= qpos[None, :, None, None, None]).reshape(B, T, G, 1, n_sel * SLC_BLOCK)
    s_s = jnp.einsum('btgid,bgtnpd->btginp', qg, kg).astype(f32) * scale - sl[..., None] * dist_s[:, :, :, None]
    p_s = masked_softmax(s_s.reshape(B, T, G, I, n_sel * SLC_BLOCK), mask_s).reshape(B, T, G, I, n_sel, SLC_BLOCK)
    o_s = jnp.einsum('btginp,bgtnpd->btgid', p_s.astype(vg.dtype), vg)
    delta = qpos[:, None] - wpos[None, :]
    mask_w = ((delta >= 0) & (delta <= WINDOW) & (wpos[None, :] >= 0))[None, :, None, None, :]
    s_w = jnp.einsum('btgid,bwgd->btgiw', qg, kw).astype(f32) * scale - sl * delta.astype(f32)[None, :, None, None, :]
    p_w = masked_softmax(s_w, mask_w)
    o_w = jnp.einsum('btgiw,bwgd->btgid', p_w.astype(vw.dtype), vw)
    o = gg[..., 0:1] * o_c + gg[..., 1:2] * o_s + gg[..., 2:3] * o_w
    return o.reshape(B, T, G * I * dh).astype(q.dtype)


def nsa_prompt(q, g, kc, vc, ks, vs, kw, vw, cmp, slopes):
    B, S = q.shape[:2]
    pos_k, pos_v, proj_k, proj_v = cmp
    kc_s = compress(pad_rows(kc, SLC_BLOCK), pos_k, proj_k)
    vc_s = compress(pad_rows(vc, SLC_BLOCK), pos_v, proj_v)
    c_end = jnp.arange(kc_s.shape[1], dtype=jnp.int32) * CMP_STRIDE + (CMP_BLOCK - 1)
    ksb = to_blocks(pad_rows(ks, SLC_BLOCK))
    vsb = to_blocks(pad_rows(vs, SLC_BLOCK))
    kw_pad = jnp.pad(kw, ((0, 0), (WINDOW, 0), (0, 0), (0, 0)))
    vw_pad = jnp.pad(vw, ((0, 0), (WINDOW, 0), (0, 0), (0, 0)))

    def one_block(start):
        qb = lax.dynamic_slice_in_dim(q, start, Q_BLOCK, axis=1)
        gb = lax.dynamic_slice_in_dim(g, start, Q_BLOCK, axis=1)
        kwb = lax.dynamic_slice_in_dim(kw_pad, start, WINDOW + Q_BLOCK, axis=1)
        vwb = lax.dynamic_slice_in_dim(vw_pad, start, WINDOW + Q_BLOCK, axis=1)
        qpos = start + jnp.arange(Q_BLOCK, dtype=jnp.int32)
        wpos = start - WINDOW + jnp.arange(WINDOW + Q_BLOCK, dtype=jnp.int32)
        return nsa_attend(qb, qpos, gb, kc_s, vc_s, c_end, ksb, vsb, kwb, vwb, wpos, slopes)

    starts = jnp.arange(S // Q_BLOCK, dtype=jnp.int32) * Q_BLOCK
    o = lax.map(one_block, starts)
    o = jnp.moveaxis(o, 0, 1).reshape(B, S, D_B)
    keep = min(WINDOW, S)
    return o, kw[:, S - keep:], vw[:, S - keep:]


def nsa_sample(q, g, kc, vc, ks, vs, kw, vw, cmp, slopes, pools, win, page_table):
    T = q.shape[1]
    pos_k, pos_v, proj_k, proj_v = cmp
    pool_kc, pool_vc, pool_ks, pool_vs = pools
    buf_kw, buf_vw = win
    past_len = page_table.shape[1] * PAGE_SIZE

    def full(pool, new):
        past = gather_pages(pool, page_table).astype(new.dtype)
        return pad_rows(jnp.concatenate([past, new], axis=1), SLC_BLOCK)

    kc_s = compress(full(pool_kc, kc), pos_k, proj_k)
    vc_s = compress(full(pool_vc, vc), pos_v, proj_v)
    c_end = jnp.arange(kc_s.shape[1], dtype=jnp.int32) * CMP_STRIDE + (CMP_BLOCK - 1)
    ksb = to_blocks(full(pool_ks, ks))
    vsb = to_blocks(full(pool_vs, vs))
    kw_all = jnp.concatenate([buf_kw.astype(kw.dtype), kw], axis=1)
    vw_all = jnp.concatenate([buf_vw.astype(vw.dtype), vw], axis=1)
    wbuf = buf_kw.shape[1]
    wpos = past_len - wbuf + jnp.arange(wbuf + T, dtype=jnp.int32)
    qpos = past_len + jnp.arange(T, dtype=jnp.int32)
    o = nsa_attend(q, qpos, g, kc_s, vc_s, c_end, ksb, vsb, kw_all, vw_all, wpos, slopes)
    keep = min(WINDOW, wbuf + T)
    return o, kw_all[:, wbuf + T - keep:], vw_all[:, wbuf + T - keep:]


def swiglu(h, w_in, w_out):
    gate, up = jnp.split(h @ w_in, 2, axis=-1)
    return (jax.nn.silu(gate) * up) @ w_out


def trunk_layer(x, lp, conv_buf, C, n, m, attend):
    (g_mix, w_in, b_in, conv_w, conv_b, m_norm, w_a, w_b, w_o, g_ffn, w_ffn_in, w_ffn_out) = lp
    B, T = x.shape[:2]
    h = rmsnorm(x, g_mix)
    proj = h @ w_in + b_in
    (m_qk, m_v, m_o, m_i, m_f, a_q, a_kc, a_vc, a_ks, a_vs, a_kw, a_vw, a_g, g_mrg) = jnp.split(proj, SPLITS, axis=-1)
    h_a, conv_new, C1, n1, m1 = mlstm_branch(m_qk, m_v, m_o, m_i, m_f, conv_buf, C, n, m, conv_w, conv_b, m_norm)

    def kv(a):
        return a.reshape(B, T, KV_GROUPS, HEAD_DIM)

    kc, vc, ks, vs, kw, vw = kv(a_kc), kv(a_vc), kv(a_ks), kv(a_vs), kv(a_kw), kv(a_vw)
    q = a_q.reshape(B, T, N_HEADS, HEAD_DIM)
    gates = jax.nn.sigmoid(a_g.astype(jnp.float32)).reshape(B, T, N_HEADS, 3)
    o_b, kw_state, vw_state = attend(q, gates, kc, vc, ks, vs, kw, vw)
    gm = jax.nn.sigmoid(g_mrg.astype(jnp.float32)).astype(x.dtype)
    y = gm[..., :D_MODEL] * (h_a @ w_a) + gm[..., D_MODEL:] * (o_b @ w_b)
    x = x + y @ w_o
    x = x + swiglu(rmsnorm(x, g_ffn), w_ffn_in, w_ffn_out)
    return x, (kc, vc, ks, vs, kw_state, vw_state, conv_new, C1, n1, m1)


def setup_inputs(seed: int = 0) -> dict:
    key = jax.random.key(seed)
    ks = jax.random.split(key, 40)
    f32 = jnp.float32

    def nrm(k, shape, s=1.0):
        return s * jax.random.normal(k, shape, f32)

    n_pages = PAST_LEN // PAGE_SIZE
    in_use = DEC_BATCH * n_pages
    n_pool = in_use + -(-in_use // 4)
    win_buf = min(WINDOW, PAST_LEN)
    page_table = jax.random.permutation(ks[0], n_pool)[:in_use].reshape(DEC_BATCH, n_pages).astype(jnp.int32)
    pool_shape = (DEPTH, n_pool, PAGE_SIZE, KV_GROUPS, HEAD_DIM)
    win_shape = (DEPTH, DEC_BATCH, win_buf, KV_GROUPS, HEAD_DIM)
    b_in = nrm(ks[1], (DEPTH, D_IN), 0.02)
    b_in = b_in.at[:, SPLITS[3]:SPLITS[4]].add(jnp.linspace(3.0, 6.0, M_HEADS, dtype=f32))
    return {
        "x_prompt": nrm(ks[2], (BATCH, SEQ, D_MODEL)),
        "x_sample": nrm(ks[3], (DEC_BATCH, DEC_SEQ, D_MODEL)),
        "cache_k_cmp": nrm(ks[4], pool_shape),
        "cache_v_cmp": nrm(ks[5], pool_shape),
        "cache_k_slc": nrm(ks[6], pool_shape),
        "cache_v_slc": nrm(ks[7], pool_shape),
        "state_k_win": nrm(ks[8], win_shape),
        "state_v_win": nrm(ks[9], win_shape),
        "state_conv": nrm(ks[10], (DEPTH, DEC_BATCH, CONV_W - 1, 2 * D_A)),
        "state_C": nrm(ks[11], (DEPTH, DEC_BATCH, M_HEADS, M_HEAD_DIM, M_HEAD_DIM), 0.1),
        "state_n": nrm(ks[12], (DEPTH, DEC_BATCH, M_HEADS, M_HEAD_DIM), 0.1),
        "state_m": nrm(ks[13], (DEPTH, DEC_BATCH, M_HEADS)),
        "page_table": page_table,
        "norm_mix": 1.0 + nrm(ks[14], (DEPTH, D_MODEL), 0.02),
        "w_in": nrm(ks[15], (DEPTH, D_MODEL, D_IN), D_MODEL ** -0.5),
        "b_in": b_in,
        "conv_w": nrm(ks[16], (DEPTH, CONV_W, 2 * D_A), CONV_W ** -0.5),
        "conv_b": nrm(ks[17], (DEPTH, 2 * D_A), 0.02),
        "cmp_pos_k": (1.0 + nrm(ks[18], (DEPTH, CMP_BLOCK, HEAD_DIM), 0.5)) * CMP_BLOCK ** -0.5,
        "cmp_pos_v": (1.0 + nrm(ks[19], (DEPTH, CMP_BLOCK, HEAD_DIM), 0.5)) * CMP_BLOCK ** -0.5,
        "cmp_proj_k": nrm(ks[20], (DEPTH, HEAD_DIM, HEAD_DIM), HEAD_DIM ** -0.5),
        "cmp_proj_v": nrm(ks[21], (DEPTH, HEAD_DIM, HEAD_DIM), HEAD_DIM ** -0.5),
        "m_norm": 1.0 + nrm(ks[22], (DEPTH, D_A), 0.02),
        "w_a": nrm(ks[23], (DEPTH, D_A, D_MODEL), D_A ** -0.5),
        "w_b": nrm(ks[24], (DEPTH, D_B, D_MODEL), D_B ** -0.5),
        "w_o": nrm(ks[25], (DEPTH, D_MODEL, D_MODEL), D_MODEL ** -0.5),
        "norm_ffn": 1.0 + nrm(ks[26], (DEPTH, D_MODEL), 0.02),
        "w_ffn_in": nrm(ks[27], (DEPTH, D_MODEL, 2 * D_FF), D_MODEL ** -0.5),
        "w_ffn_out": nrm(ks[28], (DEPTH, D_FF, D_MODEL), D_FF ** -0.5),
        "norm_final": 1.0 + nrm(ks[29], (D_MODEL,), 0.02),
    }


def reference(x_prompt, x_sample, cache_k_cmp, cache_v_cmp, cache_k_slc, cache_v_slc, state_k_win, state_v_win,
              state_conv, state_C, state_n, state_m, page_table, norm_mix, w_in, b_in, conv_w, conv_b,
              cmp_pos_k, cmp_pos_v, cmp_proj_k, cmp_proj_v, m_norm, w_a, w_b, w_o, norm_ffn, w_ffn_in,
              w_ffn_out, norm_final):
    f32 = jnp.float32
    slopes = alibi_slopes()
    yp, ys = x_prompt, x_sample
    p_states, s_states = [], []
    for l in range(DEPTH):
        lp = (norm_mix[l], w_in[l], b_in[l], conv_w[l], conv_b[l], m_norm[l], w_a[l], w_b[l], w_o[l],
              norm_ffn[l], w_ffn_in[l], w_ffn_out[l])
        cmp = (cmp_pos_k[l], cmp_pos_v[l], cmp_proj_k[l], cmp_proj_v[l])
        attend_p = functools.partial(nsa_prompt, cmp=cmp, slopes=slopes)
        conv0 = jnp.zeros((BATCH, CONV_W - 1, 2 * D_A), yp.dtype)
        C0 = jnp.zeros((BATCH, M_HEADS, M_HEAD_DIM, M_HEAD_DIM), f32)
        n0 = jnp.zeros((BATCH, M_HEADS, M_HEAD_DIM), f32)
        m0 = jnp.full((BATCH, M_HEADS), NEG, f32)
        yp, st_p = trunk_layer(yp, lp, conv0, C0, n0, m0, attend_p)
        p_states.append(st_p)
        attend_s = functools.partial(nsa_sample, cmp=cmp, slopes=slopes,
                                     pools=(cache_k_cmp[l], cache_v_cmp[l], cache_k_slc[l], cache_v_slc[l]),
                                     win=(state_k_win[l], state_v_win[l]), page_table=page_table)
        ys, st_s = trunk_layer(ys, lp, state_conv[l], state_C[l].astype(f32), state_n[l].astype(f32),
                               state_m[l].astype(f32), attend_s)
        s_states.append(st_s)
    y_prompt = rmsnorm(yp, norm_final)
    y_sample = rmsnorm(ys, norm_final)
    P = [jnp.stack(a) for a in zip(*p_states)]
    S = [jnp.stack(a) for a in zip(*s_states)]
    return (y_prompt, y_sample, P[0], S[0], P[1], S[1], P[2], S[2], P[3], S[3], P[4], S[4],
            P[5], S[5], P[6], S[6], P[7], S[7], P[8], S[8], P[9], S[9])
```

```python
import functools
import jax, jax.numpy as jnp
from jax import lax
import numpy as np
from jax.experimental import pallas as pl
from jax.experimental.pallas import tpu as pltpu

D_MODEL = 1024
BATCH = 2
SEQ = 8192
DEPTH = 2
DEC_BATCH = 32
DEC_SEQ = 4
PAST_LEN = 16384
PAGE_SIZE = 128
M_HEADS = 4
M_HEAD_DIM = D_MODEL // 8
D_A = M_HEADS * M_HEAD_DIM
CONV_W = 4
M_CHUNK = 128
N_HEADS = 8
HEAD_DIM = D_MODEL // 16
D_B = N_HEADS * HEAD_DIM
KV_GROUPS = 2
HEADS_PER_GROUP = N_HEADS // KV_GROUPS
KV_W = KV_GROUPS * HEAD_DIM
CMP_STRIDE = 16
CMP_BLOCK = 2 * CMP_STRIDE
SLC_BLOCK = 64
N_SELECT = 16
WINDOW = 512
Q_BLOCK = 128
D_FF = -(-8 * D_MODEL // (3 * 256)) * 256
RMS_EPS = 1e-6
NEG = -1e30
BIG = 1e30
SIZES = (2 * D_A, D_A, D_A, M_HEADS, M_HEADS, D_B, KV_W, KV_W, KV_W, KV_W, KV_W, KV_W, 3 * N_HEADS, 2 * D_MODEL)
SPLITS = tuple(int(s) for s in np.cumsum(SIZES)[:-1])
D_IN = sum(SIZES)

F32 = jnp.float32
BF16 = jnp.bfloat16
LANE = 128
VMEM_LIMIT = 48 * 1024 * 1024

COL_QK = 0
COL_GMA = 1024
COL_GMB = 2048
COL_V = 3072
COL_O = 3584
COL_Q = 4096
COL_KV = 5120
COL_SMALL = 5888
N_PROJ = 6144
SMALL_I = 0
SMALL_F = M_HEADS
SMALL_G = 2 * M_HEADS


def _prep_w_in(w, b):
    def cols(a):
        (m_qk, m_v, m_o, m_i, m_f, a_q, a_kc, a_vc, a_ks, a_vs, a_kw, a_vw, a_g, g_mrg) = jnp.split(a, SPLITS, axis=-1)
        z64 = jnp.zeros(a.shape[:-1] + (HEAD_DIM,), a.dtype)
        slabs = []
        for h in range(N_HEADS):
            qh = a_q[..., h * HEAD_DIM:(h + 1) * HEAD_DIM]
            slabs += [qh, z64] if h < HEADS_PER_GROUP else [z64, qh]
        small = jnp.concatenate([m_i, m_f, a_g, jnp.zeros(a.shape[:-1] + (LANE - 2 * M_HEADS - 3 * N_HEADS,), a.dtype)], -1)
        pad = jnp.zeros(a.shape[:-1] + (N_PROJ - COL_SMALL - LANE,), a.dtype)
        return jnp.concatenate([m_qk, g_mrg, m_v, m_o] + slabs + [a_kc, a_vc, a_ks, a_vs, a_kw, a_vw, small, pad], -1)
    return cols(w).astype(BF16), cols(b[None, :])


def _rms_proj_kernel(x_ref, g_ref, w_ref, b_ref, o_ref, h_scr):
    @pl.when(pl.program_id(1) == 0)
    def _():
        x = x_ref[...]
        ms = jnp.mean(x * x, axis=-1, keepdims=True)
        h_scr[...] = (x * lax.rsqrt(ms + RMS_EPS) * g_ref[...]).astype(BF16)
    o_ref[...] = jnp.dot(h_scr[...], w_ref[...], preferred_element_type=F32) + b_ref[...]


def rms_proj(x, g, w, b):
    T, D = x.shape
    N = w.shape[1]
    tm = min(T, 1024)
    tn = 512
    return pl.pallas_call(
        _rms_proj_kernel,
        grid=(T // tm, N // tn),
        in_specs=[pl.BlockSpec((tm, D), lambda i, j: (i, 0)),
                  pl.BlockSpec((1, D), lambda i, j: (0, 0)),
                  pl.BlockSpec((D, tn), lambda i, j: (0, j)),
                  pl.BlockSpec((1, tn), lambda i, j: (0, j))],
        out_specs=pl.BlockSpec((tm, tn), lambda i, j: (i, j)),
        out_shape=jax.ShapeDtypeStruct((T, N), F32),
        scratch_shapes=[pltpu.VMEM((tm, D), BF16)],
        compiler_params=pltpu.CompilerParams(dimension_semantics=("arbitrary", "arbitrary"), vmem_limit_bytes=VMEM_LIMIT),
        name="rms_proj",
    )(x, g.reshape(1, D), w, b)


def _merge_kernel(x_ref, ha_ref, ob_ref, ga_ref, gb_ref, wa_ref, wb_ref, wo_ref, o_ref):
    ya = jnp.dot(ha_ref[...].astype(BF16), wa_ref[...], preferred_element_type=F32)
    yb = jnp.dot(ob_ref[...].astype(BF16), wb_ref[...], preferred_element_type=F32)
    y = jax.nn.sigmoid(ga_ref[...]) * ya + jax.nn.sigmoid(gb_ref[...]) * yb
    o_ref[...] = x_ref[...] + jnp.dot(y.astype(BF16), wo_ref[...], preferred_element_type=F32)


def merge(x, ha, ob, proj, wa, wb, wo):
    T, D = x.shape
    tm = min(T, 512)
    row = lambda i: (i, 0)
    const = lambda i: (0, 0)
    return pl.pallas_call(
        _merge_kernel,
        grid=(T // tm,),
        in_specs=[pl.BlockSpec((tm, D), row),
                  pl.BlockSpec((tm, D_A), row),
                  pl.BlockSpec((tm, D_B), row),
                  pl.BlockSpec((tm, D), lambda i: (i, COL_GMA // D_MODEL)),
                  pl.BlockSpec((tm, D), lambda i: (i, COL_GMB // D_MODEL)),
                  pl.BlockSpec((D_A, D), const),
                  pl.BlockSpec((D_B, D), const),
                  pl.BlockSpec((D, D), const)],
        out_specs=pl.BlockSpec((tm, D), row),
        out_shape=jax.ShapeDtypeStruct((T, D), F32),
        compiler_params=pltpu.CompilerParams(dimension_semantics=("arbitrary",), vmem_limit_bytes=VMEM_LIMIT),
        name="merge",
    )(x, ha, ob, proj, proj, wa, wb, wo)


def _ffn_kernel(x_ref, g_ref, wg_ref, wu_ref, wo_ref, gf_ref, o_ref, h_scr, acc_scr, *, final_norm):
    j = pl.program_id(1)

    @pl.when(j == 0)
    def _():
        x = x_ref[...]
        ms = jnp.mean(x * x, axis=-1, keepdims=True)
        h_scr[...] = (x * lax.rsqrt(ms + RMS_EPS) * g_ref[...]).astype(BF16)
        acc_scr[...] = jnp.zeros_like(acc_scr)

    h = h_scr[...]
    gate = jnp.dot(h, wg_ref[...], preferred_element_type=F32)
    up = jnp.dot(h, wu_ref[...], preferred_element_type=F32)
    act = (gate * jax.nn.sigmoid(gate) * up).astype(BF16)
    acc_scr[...] += jnp.dot(act, wo_ref[...], preferred_element_type=F32)

    @pl.when(j == pl.num_programs(1) - 1)
    def _():
        y = x_ref[...] + acc_scr[...]
        if final_norm:
            ms = jnp.mean(y * y, axis=-1, keepdims=True)
            y = y * lax.rsqrt(ms + RMS_EPS) * gf_ref[...]
        o_ref[...] = y


def ffn(x, g, w_in, w_out, g_final, final_norm):
    T, D = x.shape
    tm = min(T, 1024)
    tf = 256
    nf = D_FF // tf
    return pl.pallas_call(
        functools.partial(_ffn_kernel, final_norm=final_norm),
        grid=(T // tm, nf),
        in_specs=[pl.BlockSpec((tm, D), lambda i, j: (i, 0)),
                  pl.BlockSpec((1, D), lambda i, j: (0, 0)),
                  pl.BlockSpec((D, tf), lambda i, j: (0, j)),
                  pl.BlockSpec((D, tf), lambda i, j: (0, j + nf)),
                  pl.BlockSpec((tf, D), lambda i, j: (j, 0)),
                  pl.BlockSpec((1, D), lambda i, j: (0, 0))],
        out_specs=pl.BlockSpec((tm, D), lambda i, j: (i, 0)),
        out_shape=jax.ShapeDtypeStruct((T, D), F32),
        scratch_shapes=[pltpu.VMEM((tm, D), BF16), pltpu.VMEM((tm, D), F32)],
        compiler_params=pltpu.CompilerParams(dimension_semantics=("arbitrary", "arbitrary"), vmem_limit_bytes=VMEM_LIMIT),
        name="ffn",
    )(x, g.reshape(1, D), w_in, w_in, w_out, g_final.reshape(1, D))


HI = lax.Precision.HIGHEST


def _mm(a, b):
    return jnp.dot(a, b, preferred_element_type=F32)


def _log_sigmoid(x):
    return jnp.minimum(x, 0.0) - jnp.log1p(jnp.exp(-jnp.abs(x)))


def _mlstm_kernel(qk_ref, v_ref, o_ref, small_ref, cbuf_ref, c0_ref, n0_ref, m0_ref, cw_ref, cb_ref, g_ref,
                  h_ref, c_out_ref, n_out_ref, m_out_ref, ext_scr, c_scr, n_scr, m_scr):
    L = M_CHUNK
    c = pl.program_id(1)

    @pl.when(c == 0)
    def _():
        ext_scr[0:8, :] = cbuf_ref[0]
        c_scr[...] = c0_ref[0]
        n_scr[...] = n0_ref[0]
        m_scr[...] = m0_ref[0]

    ext_scr[8:8 + L, :] = qk_ref[...]
    conv = cb_ref[...] + ext_scr[5:5 + L, :] * cw_ref[0:1, :]
    for j in range(1, CONV_W):
        conv = conv + ext_scr[5 + j:5 + j + L, :] * cw_ref[j:j + 1, :]
    ext_scr[0:8, :] = qk_ref[L - 8:L, :]
    qk = conv * jax.nn.sigmoid(conv)

    small = small_ref[...]
    small_t = small.T
    row = lax.broadcasted_iota(jnp.int32, (L, L), 0)
    col = lax.broadcasted_iota(jnp.int32, (L, L), 1)
    causal = col <= row
    fcol_all = jnp.dot(causal.astype(F32), _log_sigmoid(small), preferred_element_type=F32, precision=HI)
    frow_all = jnp.dot(_log_sigmoid(small_t), (row <= col).astype(F32), preferred_element_type=F32, precision=HI)

    for h in range(M_HEADS):
        sl = slice(h * M_HEAD_DIM, (h + 1) * M_HEAD_DIM)
        q = qk[:, sl]
        k = qk[:, D_A + h * M_HEAD_DIM:D_A + (h + 1) * M_HEAD_DIM] * (M_HEAD_DIM ** -0.5)
        fc = fcol_all[:, SMALL_F + h:SMALL_F + h + 1]
        fr = frow_all[SMALL_F + h:SMALL_F + h + 1, :]
        igr = small_t[SMALL_I + h:SMALL_I + h + 1, :]
        igc = small[:, SMALL_I + h:SMALL_I + h + 1]
        m_prev = m_scr[h:h + 1, 0:1]
        d = jnp.where(causal, fc - fr + igr, NEG)
        a = fc + m_prev
        m_t = jnp.maximum(a, jnp.max(d, axis=-1, keepdims=True))
        qb = q.astype(BF16)
        kb = k.astype(BF16)
        vb = v_ref[:, sl].astype(BF16)
        s = lax.dot_general(qb, kb, (((1,), (1,)), ((), ())), preferred_element_type=F32) * jnp.exp(d - m_t)
        inter = jnp.exp(a - m_t)
        num = _mm(s.astype(BF16), vb) + inter * _mm(qb, c_scr[h].astype(BF16))
        den = jnp.sum(s, axis=-1, keepdims=True) + inter * jnp.sum(q * n_scr[h:h + 1, :], axis=-1, keepdims=True)
        hh = num / jnp.maximum(jnp.abs(den), jnp.exp(-m_t))
        mu = jnp.mean(hh, axis=-1, keepdims=True)
        var = jnp.mean(jnp.square(hh - mu), axis=-1, keepdims=True)
        hn = (hh - mu) * lax.rsqrt(var + RMS_EPS) * g_ref[:, sl]
        h_ref[:, sl] = jax.nn.sigmoid(o_ref[:, sl]) * hn
        f_l = fr[:, L - 1:L]
        m_new = jnp.maximum(f_l + m_prev, jnp.max(f_l - fr + igr, axis=-1, keepdims=True))
        decay = jnp.exp(f_l + m_prev - m_new)
        ks = k * jnp.exp(f_l - fc + igc - m_new)
        c_scr[h] = decay * c_scr[h] + lax.dot_general(ks.astype(BF16), vb, (((0,), (0,)), ((), ())), preferred_element_type=F32)
        n_scr[h:h + 1, :] = decay * n_scr[h:h + 1, :] + jnp.sum(ks, axis=0, keepdims=True)
        m_scr[h:h + 1, :] = jnp.broadcast_to(m_new, (1, LANE))

    @pl.when(c == pl.num_programs(1) - 1)
    def _():
        c_out_ref[0] = c_scr[...]
        n_out_ref[0] = n_scr[...]
        m_out_ref[0] = m_scr[...]


def mlstm(B, T, qk, qk_col, v, v_col, o, o_col, small, small_col, conv_buf, C0, n0, m0, conv_w, conv_b, head_g):
    L = M_CHUNK
    nch = T // L
    cbuf = jnp.concatenate([jnp.zeros((B, 8 - (CONV_W - 1), 2 * D_A), F32), conv_buf], axis=1)
    n0p = jnp.concatenate([n0, jnp.zeros((B, 8 - M_HEADS, M_HEAD_DIM), F32)], axis=1)
    m0p = jnp.broadcast_to(jnp.concatenate([m0, jnp.zeros((B, 8 - M_HEADS), F32)], axis=1)[:, :, None], (B, 8, LANE))
    tok = lambda cb: (lambda b, c: (b * nch + c, cb))
    per_b3 = lambda b, c: (b, 0, 0)
    per_b4 = lambda b, c: (b, 0, 0, 0)
    const = lambda b, c: (0, 0)
    h, C1, n1, m1 = pl.pallas_call(
        _mlstm_kernel,
        grid=(B, nch),
        in_specs=[pl.BlockSpec((L, 2 * D_A), tok(qk_col)),
                  pl.BlockSpec((L, D_A), tok(v_col)),
                  pl.BlockSpec((L, D_A), tok(o_col)),
                  pl.BlockSpec((L, LANE), tok(small_col)),
                  pl.BlockSpec((1, 8, 2 * D_A), per_b3),
                  pl.BlockSpec((1, M_HEADS, M_HEAD_DIM, M_HEAD_DIM), per_b4),
                  pl.BlockSpec((1, 8, M_HEAD_DIM), per_b3),
                  pl.BlockSpec((1, 8, LANE), per_b3),
                  pl.BlockSpec((CONV_W, 2 * D_A), const),
                  pl.BlockSpec((1, 2 * D_A), const),
                  pl.BlockSpec((1, D_A), const)],
        out_specs=[pl.BlockSpec((L, D_A), tok(0)),
                   pl.BlockSpec((1, M_HEADS, M_HEAD_DIM, M_HEAD_DIM), per_b4),
                   pl.BlockSpec((1, 8, M_HEAD_DIM), per_b3),
                   pl.BlockSpec((1, 8, LANE), per_b3)],
        out_shape=[jax.ShapeDtypeStruct((B * T, D_A), F32),
                   jax.ShapeDtypeStruct((B, M_HEADS, M_HEAD_DIM, M_HEAD_DIM), F32),
                   jax.ShapeDtypeStruct((B, 8, M_HEAD_DIM), F32),
                   jax.ShapeDtypeStruct((B, 8, LANE), F32)],
        scratch_shapes=[pltpu.VMEM((8 + L, 2 * D_A), F32),
                        pltpu.VMEM((M_HEADS, M_HEAD_DIM, M_HEAD_DIM), F32),
                        pltpu.VMEM((8, M_HEAD_DIM), F32),
                        pltpu.VMEM((8, LANE), F32)],
        compiler_params=pltpu.CompilerParams(dimension_semantics=("arbitrary", "arbitrary"), vmem_limit_bytes=VMEM_LIMIT),
        name="mlstm",
    )(qk, v, o, small, cbuf, C0, n0p, m0p, conv_w, conv_b.reshape(1, -1), head_g.reshape(1, -1))
    return h, C1, n1[:, :M_HEADS], m1[:, :M_HEADS, 0]


SLOPES = tuple(float(2.0 ** (-8.0 * (h + 1) / N_HEADS)) for h in range(N_HEADS))
Q_SCALE = HEAD_DIM ** -0.5
REMOVED = -3.0e38


def _pool(x, w_ref, n):
    x3 = x.reshape(n, CMP_STRIDE, LANE)
    head = jnp.sum(x3 * w_ref[0:CMP_STRIDE, :][None], axis=1)
    tail = jnp.sum(x3 * w_ref[CMP_STRIDE:CMP_BLOCK, :][None], axis=1)
    nxt = pltpu.roll(tail, n - 1, 0)
    last = lax.broadcasted_iota(jnp.int32, (n, LANE), 0) == n - 1
    return head + jnp.where(last, 0.0, nxt)


def _compress_kernel(k_ref, v_ref, wk_ref, wv_ref, pk_ref, pv_ref, kc_ref, vct_ref, *, n):
    pk = _pool(k_ref[...], wk_ref, n)
    pv = _pool(v_ref[...], wv_ref, n)
    kc_ref[0] = _mm(pk.astype(BF16), pk_ref[...]).astype(BF16)
    vct_ref[0] = _mm(pv.astype(BF16), pv_ref[...]).T.astype(BF16)


def _cmp_weights(pos_k, pos_v, proj_k, proj_v):
    tile = lambda w: jnp.concatenate([w] * KV_GROUPS, axis=1)
    bd = lambda w: jnp.kron(jnp.eye(KV_GROUPS, dtype=F32), w).astype(BF16)
    return tile(pos_k), tile(pos_v), bd(proj_k), bd(proj_v)


def compress_prompt(B, S, kc, kc_col, vc, vc_col, cw):
    n = S // CMP_STRIDE
    wk, wv, pk, pv = cw
    const = lambda b: (0, 0)
    return pl.pallas_call(
        functools.partial(_compress_kernel, n=n),
        grid=(B,),
        in_specs=[pl.BlockSpec((S, LANE), lambda b: (b, kc_col)),
                  pl.BlockSpec((S, LANE), lambda b: (b, vc_col)),
                  pl.BlockSpec((CMP_BLOCK, LANE), const), pl.BlockSpec((CMP_BLOCK, LANE), const),
                  pl.BlockSpec((LANE, LANE), const), pl.BlockSpec((LANE, LANE), const)],
        out_specs=[pl.BlockSpec((1, n, LANE), lambda b: (b, 0, 0)), pl.BlockSpec((1, LANE, n), lambda b: (b, 0, 0))],
        out_shape=[jax.ShapeDtypeStruct((B, n, LANE), BF16), jax.ShapeDtypeStruct((B, LANE, n), BF16)],
        compiler_params=pltpu.CompilerParams(dimension_semantics=("arbitrary",), vmem_limit_bytes=VMEM_LIMIT),
        name="compress_prompt",
    )(kc, vc, wk, wv, pk, pv)


def _nsa_prompt_kernel(q_ref, small_ref, kc_ref, vct_ref, ks_ref, vs_ref, kw_ref, vw_ref, o_ref,
                       ksb, vstb, kwb, vwtb, qt_scr, sd_scr, p_scr, sel_scr, m_scr, l_scr, acc_scr, oc_scr, *, S):
    QB = Q_BLOCK
    nkb = S // QB
    ncb = S // CMP_STRIDE
    nsb = S // SLC_BLOCK
    G, I, HD = KV_GROUPS, HEADS_PER_GROUP, HEAD_DIM
    qb = pl.program_id(1)
    c_i = lax.broadcasted_iota(jnp.int32, (QB, QB), 0)
    r_i = lax.broadcasted_iota(jnp.int32, (QB, QB), 1)

    @pl.when(qb == 0)
    def _():
        def cp(i, carry):
            rows = pl.ds(pl.multiple_of(i * QB, QB), QB)
            ksb[i] = ks_ref[rows, :].astype(BF16)
            kwb[i] = kw_ref[rows, :].astype(BF16)
            vstb[i] = vs_ref[rows, :].T.astype(BF16)
            vwtb[i] = vw_ref[rows, :].T.astype(BF16)
            return carry
        lax.fori_loop(0, nkb, cp, 0)
        rc = (r_i - c_i).astype(F32)
        for h in range(N_HEADS):
            sd_scr[h] = SLOPES[h] * rc
        p_scr[0:8, :] = jnp.zeros((8, LANE), F32)
        p_scr[8 + ncb:16 + ncb, :] = jnp.zeros((8, LANE), F32)

    for h in range(N_HEADS):
        g, i = divmod(h, I)
        qt_scr[g, :, i * QB:(i + 1) * QB] = (q_ref[:, h * LANE:(h + 1) * LANE] * Q_SCALE).T.astype(BF16)
    small_t = small_ref[...].T
    gates_t = jax.nn.sigmoid(small_t[SMALL_G:SMALL_G + 3 * N_HEADS, :])
    m_scr[...] = jnp.full(m_scr.shape, NEG, F32)
    l_scr[...] = jnp.zeros(l_scr.shape, F32)
    acc_scr[...] = jnp.zeros(acc_scr.shape, F32)

    n_i = lax.broadcasted_iota(jnp.int32, (ncb, QB), 0)
    rr_i = lax.broadcasted_iota(jnp.int32, (ncb, QB), 1)
    dist_c = (rr_i - CMP_STRIDE * n_i - (CMP_BLOCK - 1) + QB * qb).astype(F32)
    mask_c = dist_c >= 0.0
    j_i = lax.broadcasted_iota(jnp.int32, (nsb, QB), 0)
    tq = lax.broadcasted_iota(jnp.int32, (nsb, QB), 1) + QB * qb
    cur = lax.shift_right_arithmetic(tq, 6)
    forced = (j_i == 0) | (j_i == cur) | (j_i == cur - 1)
    valid = j_i * SLC_BLOCK <= tq
    jf = j_i.astype(F32)
    for g in range(G):
        s4 = _mm(kc_ref[0], qt_scr[g])
        psum = jnp.zeros((ncb, QB), F32)
        pns = []
        for i in range(I):
            h = g * I + i
            s = jnp.where(mask_c, s4[:, i * QB:(i + 1) * QB] - SLOPES[h] * dist_c, NEG)
            e = jnp.exp(s - jnp.max(s, axis=0, keepdims=True))
            pn = jnp.where(mask_c, e, 0.0) / jnp.sum(e, axis=0, keepdims=True)
            psum = psum + pn
            pns.append(pn.astype(BF16))
        oc_scr[g] = _mm(vct_ref[0][g * HD:(g + 1) * HD, :], jnp.concatenate(pns, axis=1))
        p_scr[8:8 + ncb, :] = psum
        imp = p_scr[pl.ds(7, nsb, stride=4), :]
        for d in range(1, 5):
            imp = imp + p_scr[pl.ds(7 + d, nsb, stride=4), :]
        score = jnp.where(forced, BIG, jnp.where(valid, imp, NEG))
        sel = jnp.zeros((nsb, QB), F32)
        for _ in range(min(N_SELECT, nsb)):
            cm = jnp.max(score, axis=0, keepdims=True)
            idx = jnp.min(jnp.where(score == cm, jf, 1e9), axis=0, keepdims=True)
            hit = jf == idx
            sel = jnp.where(hit, 1.0, sel)
            score = jnp.where(hit, REMOVED, score)
        sel_scr[g] = sel

    def update(slot, g, s4, mk, off, vt_g):
        m_old = m_scr[slot, 0:1, :]
        l_old = l_scr[slot, 0:1, :]
        ps, m_new, l_new, alphas = [], [], [], []
        for i in range(I):
            h = g * I + i
            s = s4[:, i * QB:(i + 1) * QB] - sd_scr[h]
            if mk is not None:
                s = jnp.where(mk, s, NEG)
            so = SLOPES[h] * off
            mo = m_old[:, i * QB:(i + 1) * QB]
            mn = jnp.maximum(mo, jnp.max(s, axis=0, keepdims=True) - so)
            p = jnp.exp(s - (mn + so))
            al = jnp.exp(mo - mn)
            ps.append(p.astype(BF16))
            m_new.append(mn)
            alphas.append(al)
            l_new.append(al * l_old[:, i * QB:(i + 1) * QB] + jnp.sum(p, axis=0, keepdims=True))
        acc_scr[slot] = jnp.concatenate(alphas, axis=1) * acc_scr[slot] + _mm(vt_g, jnp.concatenate(ps, axis=1))
        m_scr[slot, 0:1, :] = jnp.concatenate(m_new, axis=1)
        l_scr[slot, 0:1, :] = jnp.concatenate(l_new, axis=1)

    def sel_block(kb, diag):
        kblk = ksb[kb]
        vt = vstb[kb]
        off = (qb - kb).astype(F32) * float(QB)
        for g in range(G):
            s4 = _mm(kblk, qt_scr[g])
            m0 = sel_scr[g, pl.ds(2 * kb, 1), :]
            m1 = sel_scr[g, pl.ds(2 * kb + 1, 1), :]
            mk = jnp.concatenate([jnp.broadcast_to(m0, (SLC_BLOCK, QB)), jnp.broadcast_to(m1, (SLC_BLOCK, QB))], axis=0) > 0.5
            if diag:
                mk = mk & (c_i <= r_i)
            update(g, g, s4, mk, off, vt[g * HD:(g + 1) * HD, :])

    def sel_body(kb, carry):
        sel_block(kb, False)
        return carry
    lax.fori_loop(0, qb, sel_body, 0)
    sel_block(qb, True)

    nwb = WINDOW // QB
    for d in range(nwb + 1):
        def win_block(d=d):
            kb = qb - nwb + d
            kblk = kwb[kb]
            vt = vwtb[kb]
            mk = (r_i <= c_i) if d == 0 else ((c_i <= r_i) if d == nwb else None)
            for g in range(G):
                update(G + g, g, _mm(kblk, qt_scr[g]), mk, float((nwb - d) * QB), vt[g * HD:(g + 1) * HD, :])
        if d == nwb:
            win_block()
        else:
            pl.when(qb - nwb + d >= 0)(win_block)

    for pair in range(N_HEADS // 2):
        halves = []
        for h in (2 * pair, 2 * pair + 1):
            g, i = divmod(h, I)
            cols = slice(i * QB, (i + 1) * QB)
            o_s = acc_scr[g, :, cols] / l_scr[g, 0:1, cols]
            o_w = acc_scr[G + g, :, cols] / l_scr[G + g, 0:1, cols]
            o_c = oc_scr[g, :, cols]
            halves.append(gates_t[3 * h:3 * h + 1, :] * o_c + gates_t[3 * h + 1:3 * h + 2, :] * o_s + gates_t[3 * h + 2:3 * h + 3, :] * o_w)
        o_ref[:, pair * LANE:(pair + 1) * LANE] = jnp.concatenate(halves, axis=0).T


def nsa_prompt_attn(B, S, proj, kc_s, vc_st):
    nq = S // Q_BLOCK
    ncb = S // CMP_STRIDE
    nsb = S // SLC_BLOCK
    tok = lambda cb: (lambda b, q: (b * nq + q, cb))
    seq = lambda cb: (lambda b, q: (b, cb))
    one = pl.Buffered(1)
    kvc = COL_KV // LANE
    w4 = HEADS_PER_GROUP * Q_BLOCK
    return pl.pallas_call(
        functools.partial(_nsa_prompt_kernel, S=S),
        grid=(B, nq),
        in_specs=[pl.BlockSpec((Q_BLOCK, N_HEADS * LANE), tok(COL_Q // (N_HEADS * LANE))),
                  pl.BlockSpec((Q_BLOCK, LANE), tok(COL_SMALL // LANE)),
                  pl.BlockSpec((1, ncb, LANE), lambda b, q: (b, 0, 0)),
                  pl.BlockSpec((1, LANE, ncb), lambda b, q: (b, 0, 0)),
                  pl.BlockSpec((S, LANE), seq(kvc + 2), pipeline_mode=one),
                  pl.BlockSpec((S, LANE), seq(kvc + 3), pipeline_mode=one),
                  pl.BlockSpec((S, LANE), seq(kvc + 4), pipeline_mode=one),
                  pl.BlockSpec((S, LANE), seq(kvc + 5), pipeline_mode=one)],
        out_specs=pl.BlockSpec((Q_BLOCK, D_B), tok(0)),
        out_shape=jax.ShapeDtypeStruct((B * S, D_B), F32),
        scratch_shapes=[pltpu.VMEM((nq, Q_BLOCK, LANE), BF16), pltpu.VMEM((nq, LANE, Q_BLOCK), BF16),
                        pltpu.VMEM((nq, Q_BLOCK, LANE), BF16), pltpu.VMEM((nq, LANE, Q_BLOCK), BF16),
                        pltpu.VMEM((KV_GROUPS, LANE, w4), BF16),
                        pltpu.VMEM((N_HEADS, Q_BLOCK, Q_BLOCK), F32),
                        pltpu.VMEM((ncb + 16, Q_BLOCK), F32),
                        pltpu.VMEM((KV_GROUPS, nsb, Q_BLOCK), F32),
                        pltpu.VMEM((2 * KV_GROUPS, 8, w4), F32),
                        pltpu.VMEM((2 * KV_GROUPS, 8, w4), F32),
                        pltpu.VMEM((2 * KV_GROUPS, HEAD_DIM, w4), F32),
                        pltpu.VMEM((KV_GROUPS, HEAD_DIM, w4), F32)],
        compiler_params=pltpu.CompilerParams(dimension_semantics=("arbitrary", "arbitrary"), vmem_limit_bytes=VMEM_LIMIT),
        name="nsa_prompt",
    )(proj, proj, kc_s, vc_st, proj, proj, proj, proj)


N_PAGES = PAST_LEN // PAGE_SIZE
S_NCB = PAST_LEN // CMP_STRIDE
S_NSB = (PAST_LEN + SLC_BLOCK) // SLC_BLOCK
S_NSB_PAD = -(-S_NSB // 8) * 8
PAGES_PER_STEP = 16
CMP_STEPS = N_PAGES // PAGES_PER_STEP
NCOL = DEC_SEQ * N_HEADS
N_DYN = N_SELECT - 3
DYN_ROWS = N_DYN * SLC_BLOCK
ROW_DYN0 = 2 * SLC_BLOCK
ROW_NEW = ROW_DYN0 + DEC_SEQ * KV_GROUPS * DYN_ROWS
NEW_PAD = LANE
ROWS_ALL = ROW_NEW + NEW_PAD
WIN_BUF = min(WINDOW, PAST_LEN)
WIN_ROWS = WIN_BUF + NEW_PAD
assert ROWS_ALL % LANE == 0 and WIN_ROWS % LANE == 0
assert (PAST_LEN + DEC_SEQ - 1) // SLC_BLOCK == S_NSB - 1 and PAST_LEN % SLC_BLOCK == 0 and DEC_SEQ <= 8
assert CMP_STRIDE * S_NCB + CMP_BLOCK - 1 > PAST_LEN + DEC_SEQ - 1


def _col_consts():
    c = np.arange(LANE)
    t = c // N_HEADS
    h = c % N_HEADS
    ok = c < NCOL
    info = np.zeros((8, LANE), np.float32)
    info[0] = np.where(ok, 2.0 ** (-8.0 * (h + 1) / N_HEADS), 0.0)
    info[1] = np.where(ok, PAST_LEN + t, 0)
    info[2] = t
    info[3] = ok
    return info, t, h // HEADS_PER_GROUP, ok


def _sel_mask():
    _, t, g, ok = _col_consts()
    m = np.zeros((ROWS_ALL, LANE), np.float32)
    m[0:ROW_DYN0] = ok[None, :]
    for tt in range(DEC_SEQ):
        for gg in range(KV_GROUPS):
            r0 = ROW_DYN0 + (tt * KV_GROUPS + gg) * DYN_ROWS
            m[r0:r0 + DYN_ROWS] = (ok & (t == tt) & (g == gg))[None, :]
    r = np.arange(NEW_PAD)
    m[ROW_NEW:ROW_NEW + NEW_PAD] = (ok[None, :] & (r[:, None] <= t[None, :]) & (r[:, None] < DEC_SEQ))
    return m


def _compress_paged_kernel(pt_ref, *refs):
    P = PAGES_PER_STEP
    k_pages, k_next = refs[0:P], refs[P]
    v_pages, v_next = refs[P + 1:2 * P + 1], refs[2 * P + 1]
    wk_ref, wv_ref, pk_ref, pv_ref, kc_ref, vct_ref = refs[2 * P + 2:]
    n = P * PAGE_SIZE // CMP_STRIDE
    not_last = (pl.program_id(1) < pl.num_programs(1) - 1).astype(F32)
    last_row = lax.broadcasted_iota(jnp.int32, (n, LANE), 0) == n - 1

    def pooled(pages, nxt, w_ref):
        x = jnp.concatenate([p[0, 0] for p in pages], axis=0)
        tail_next = jnp.sum(nxt[0, 0] * w_ref[CMP_STRIDE:CMP_BLOCK, :], axis=0, keepdims=True) * not_last
        return _pool(x, w_ref, n) + jnp.where(last_row, tail_next, 0.0)

    kc_ref[0] = _mm(pooled(k_pages, k_next, wk_ref).astype(BF16), pk_ref[...]).astype(BF16)
    vct_ref[0] = _mm(pooled(v_pages, v_next, wv_ref).astype(BF16), pv_ref[...]).T.astype(BF16)


def compress_paged(l, pool_k, pool_v, page_table, cw):
    B = page_table.shape[0]
    P = PAGES_PER_STEP
    wk, wv, pk, pv = cw
    n = P * PAGE_SIZE // CMP_STRIDE
    page = lambda i: pl.BlockSpec((1, 1, PAGE_SIZE, LANE), lambda b, j, pt: (l, pt[b, j * P + i], 0, 0))
    nxt = pl.BlockSpec((1, 1, CMP_STRIDE, LANE), lambda b, j, pt: (l, pt[b, jnp.minimum(j * P + P, N_PAGES - 1)], 0, 0))
    const = lambda b, j, pt: (0, 0)
    wspecs = [pl.BlockSpec((CMP_BLOCK, LANE), const), pl.BlockSpec((CMP_BLOCK, LANE), const),
              pl.BlockSpec((LANE, LANE), const), pl.BlockSpec((LANE, LANE), const)]
    gs = pltpu.PrefetchScalarGridSpec(
        num_scalar_prefetch=1, grid=(B, CMP_STEPS),
        in_specs=[page(i) for i in range(P)] + [nxt] + [page(i) for i in range(P)] + [nxt] + wspecs,
        out_specs=[pl.BlockSpec((1, n, LANE), lambda b, j, pt: (b, j, 0)), pl.BlockSpec((1, LANE, n), lambda b, j, pt: (b, 0, j))])
    return pl.pallas_call(
        _compress_paged_kernel, grid_spec=gs,
        out_shape=[jax.ShapeDtypeStruct((B, S_NCB, LANE), BF16), jax.ShapeDtypeStruct((B, LANE, S_NCB), BF16)],
        compiler_params=pltpu.CompilerParams(dimension_semantics=("arbitrary", "arbitrary"), vmem_limit_bytes=VMEM_LIMIT),
        name="compress_paged",
    )(page_table, *([pool_k] * (P + 1)), *([pool_v] * (P + 1)), wk, wv, pk, pv)


def _query_cols(q32):
    qp = jnp.concatenate([q32, jnp.zeros((LANE - NCOL, LANE), F32)], axis=0)
    return (qp * Q_SCALE).T.astype(BF16)


def _own_group_lanes(o):
    lane_g = lax.broadcasted_iota(jnp.int32, (NCOL, LANE), 1) // HEAD_DIM
    row_g = (lax.broadcasted_iota(jnp.int32, (NCOL, LANE), 0) % N_HEADS) // HEADS_PER_GROUP
    return jnp.where(lane_g == row_g, o, 0.0)


def _sample_cmp_kernel(q_ref, kc_ref, vct_ref, info_ref, grp_ref, oc_ref, idx_ref, p_scr):
    w = _query_cols(q_ref[0])
    slope = info_ref[0:1, :]
    qpos = info_ref[1:2, :]
    n_f = lax.broadcasted_iota(jnp.int32, (S_NCB, LANE), 0).astype(F32)
    dist = qpos - (CMP_STRIDE * n_f + (CMP_BLOCK - 1))
    mask = dist >= 0.0
    s = jnp.where(mask, _mm(kc_ref[0], w) - slope * dist, NEG)
    e = jnp.exp(s - jnp.max(s, axis=0, keepdims=True))
    pn = jnp.where(mask, e, 0.0) / jnp.sum(e, axis=0, keepdims=True)
    oc_ref[0] = _own_group_lanes(_mm(vct_ref[0], pn.astype(BF16)).T[0:NCOL, :])
    pg = jnp.dot(pn, grp_ref[...], preferred_element_type=F32, precision=HI)
    p_scr[0:8, :] = jnp.zeros((8, LANE), F32)
    p_scr[8:8 + S_NCB, :] = pg
    p_scr[8 + S_NCB:, :] = jnp.zeros((p_scr.shape[0] - 8 - S_NCB, LANE), F32)
    imp = p_scr[pl.ds(7, S_NSB_PAD, stride=4), :]
    for d in range(1, 5):
        imp = imp + p_scr[pl.ds(7 + d, S_NSB_PAD, stride=4), :]
    j_i = lax.broadcasted_iota(jnp.int32, (S_NSB_PAD, LANE), 0)
    tq = PAST_LEN + lax.broadcasted_iota(jnp.int32, (S_NSB_PAD, LANE), 1) // KV_GROUPS
    cur = lax.shift_right_arithmetic(tq, 6)
    forced = (j_i == 0) | (j_i == cur) | (j_i == cur - 1)
    valid = j_i * SLC_BLOCK <= tq
    score = jnp.where(j_i >= S_NSB, REMOVED, jnp.where(forced, BIG, jnp.where(valid, imp, NEG)))
    jf = j_i.astype(F32)
    picks = []
    for _ in range(N_SELECT):
        cm = jnp.max(score, axis=0, keepdims=True)
        idx = jnp.min(jnp.where(score == cm, jf, 1e9), axis=0, keepdims=True)
        score = jnp.where(jf == idx, REMOVED, score)
        picks.append(idx)
    idx_ref[0] = jnp.concatenate(picks, axis=0).astype(jnp.int32)


def sample_cmp(q32, kc_s, vc_st):
    B = q32.shape[0]
    info, t, g, ok = _col_consts()
    grp = np.zeros((LANE, LANE), np.float32)
    for c in range(NCOL):
        grp[c, t[c] * KV_GROUPS + g[c]] = 1.0
    per_b = lambda b: (b, 0, 0)
    const = lambda b: (0, 0)
    return pl.pallas_call(
        _sample_cmp_kernel, grid=(B,),
        in_specs=[pl.BlockSpec((1, NCOL, LANE), per_b), pl.BlockSpec((1, S_NCB, LANE), per_b), pl.BlockSpec((1, LANE, S_NCB), per_b),
                  pl.BlockSpec((8, LANE), const), pl.BlockSpec((LANE, LANE), const)],
        out_specs=[pl.BlockSpec((1, NCOL, LANE), per_b), pl.BlockSpec((1, N_SELECT, LANE), per_b)],
        out_shape=[jax.ShapeDtypeStruct((B, NCOL, LANE), F32), jax.ShapeDtypeStruct((B, N_SELECT, LANE), jnp.int32)],
        scratch_shapes=[pltpu.VMEM((8 + 4 * S_NSB_PAD + 8, LANE), F32)],
        compiler_params=pltpu.CompilerParams(dimension_semantics=("arbitrary",), vmem_limit_bytes=VMEM_LIMIT),
        name="sample_cmp",
    )(q32, kc_s, vc_st, jnp.asarray(info), jnp.asarray(grp))


def _sample_gather_kernel(pt_ref, idx_ref, *refs):
    k_blk, v_blk = refs[0:N_DYN], refs[N_DYN:2 * N_DYN]
    kd_ref, vd_ref = refs[2 * N_DYN:]
    for k in range(N_DYN):
        rows = slice(k * SLC_BLOCK, (k + 1) * SLC_BLOCK)
        kd_ref[0, 0, rows, :] = k_blk[k][0, 0].astype(BF16)
        vd_ref[0, 0, rows, :] = v_blk[k][0, 0].astype(BF16)


def sample_gather(l, pool_k, pool_v, page_table, idx):
    B = page_table.shape[0]
    ntg = DEC_SEQ * KV_GROUPS
    per_page = PAGE_SIZE // SLC_BLOCK

    def dyn(k):
        def im(b, tg, pt, ix):
            blk = jnp.minimum(ix[b, (k + 3) * ntg + tg], S_NSB - 2)
            return (l, pt[b, blk // per_page] * per_page + blk % per_page, 0, 0)
        return pl.BlockSpec((1, 1, SLC_BLOCK, LANE), im)
    specs = [dyn(k) for k in range(N_DYN)]
    out = pl.BlockSpec((1, 1, DYN_ROWS, LANE), lambda b, tg, pt, ix: (b, tg, 0, 0))
    gs = pltpu.PrefetchScalarGridSpec(num_scalar_prefetch=2, grid=(B, ntg), in_specs=specs + specs, out_specs=[out, out])
    shape = jax.ShapeDtypeStruct((B, ntg, DYN_ROWS, LANE), BF16)
    return pl.pallas_call(
        _sample_gather_kernel, grid_spec=gs, out_shape=[shape, shape],
        compiler_params=pltpu.CompilerParams(dimension_semantics=("arbitrary", "arbitrary"), vmem_limit_bytes=VMEM_LIMIT),
        name="sample_gather",
    )(page_table, idx, *([pool_k] * N_DYN), *([pool_v] * N_DYN))


def _sample_sel_kernel(pt_ref, idx_ref, kd_ref, vd_ref, k0_ref, k1_ref, v0_ref, v1_ref, knew_ref, vnew_ref, kwb_ref, vwb_ref,
                       kwn_ref, vwn_ref, q_ref, gate_ref, oc_ref, info_ref, mask_ref, o_ref, kpos):
    b = pl.program_id(0)
    w = _query_cols(q_ref[0])
    slope = info_ref[0:1, :]
    qpos = info_ref[1:2, :]
    tcol = info_ref[2:3, :]
    okc = info_ref[3:4, :]
    r_f = lax.broadcasted_iota(jnp.int32, (NEW_PAD, LANE), 0).astype(F32)
    cur = (PAST_LEN + DEC_SEQ - 1) // SLC_BLOCK
    ntg = DEC_SEQ * KV_GROUPS
    zpad = jnp.zeros((NEW_PAD - 8, LANE), F32)

    kpos[0:SLC_BLOCK, :] = r_f[0:SLC_BLOCK, :]
    kpos[SLC_BLOCK:ROW_DYN0, :] = float((cur - 1) * SLC_BLOCK) + r_f[0:SLC_BLOCK, :]
    for tg in range(ntg):
        for k in range(N_DYN):
            r0 = ROW_DYN0 + (tg * N_DYN + k) * SLC_BLOCK
            kpos[r0:r0 + SLC_BLOCK, :] = (idx_ref[b, (k + 3) * ntg + tg] * SLC_BLOCK).astype(F32) + r_f[0:SLC_BLOCK, :]
    kpos[ROW_NEW:ROWS_ALL, :] = float(cur * SLC_BLOCK) + r_f

    def gathered(first, prev, dyn_ref, new_ref):
        parts = [first[0, 0].astype(BF16), prev[0, 0].astype(BF16)] + [dyn_ref[0, tg] for tg in range(ntg)]
        return jnp.concatenate(parts + [jnp.concatenate([new_ref[0], zpad], axis=0).astype(BF16)], axis=0)

    def attend(k_rows, v_rows, mask, dist):
        s = jnp.where(mask, _mm(k_rows, w) - slope * dist, NEG)
        p = jnp.exp(s - jnp.max(s, axis=0, keepdims=True))
        l = jnp.sum(p, axis=0, keepdims=True)
        o_t = lax.dot_general(v_rows, p.astype(BF16), (((0,), (0,)), ((), ())), preferred_element_type=F32)
        return (o_t / l).T[0:NCOL, :]

    o_s = attend(gathered(k0_ref, k1_ref, kd_ref, knew_ref), gathered(v0_ref, v1_ref, vd_ref, vnew_ref),
                 mask_ref[...] > 0.5, qpos - kpos[...])
    i_w = lax.broadcasted_iota(jnp.int32, (WIN_ROWS, LANE), 0).astype(F32)
    mask_w = (i_w >= tcol) & (i_w <= tcol + float(WIN_BUF)) & (okc > 0.5)
    kw_rows = jnp.concatenate([kwb_ref[0, 0], kwn_ref[0], zpad], axis=0).astype(BF16)
    vw_rows = jnp.concatenate([vwb_ref[0, 0], vwn_ref[0], zpad], axis=0).astype(BF16)
    o_w = attend(kw_rows, vw_rows, mask_w, float(WIN_BUF) + tcol - i_w)
    gates = jax.nn.sigmoid(gate_ref[0])
    o_ref[0] = _own_group_lanes(gates[:, 0:1] * oc_ref[0] + gates[:, 1:2] * o_s + gates[:, 2:3] * o_w)


def sample_sel_dense(l, pool_k, pool_v, page_table, idx, knew, vnew, kwin, vwin, kwnew, vwnew, q32, gate32, oc):
    B = page_table.shape[0]
    info, _, _, _ = _col_consts()
    ntg = DEC_SEQ * KV_GROUPS
    per_page = PAGE_SIZE // SLC_BLOCK
    kd, vd = sample_gather(l, pool_k, pool_v, page_table, idx)
    per_b = lambda b, pt, ix: (b, 0, 0)
    const = lambda b, pt, ix: (0, 0)
    dense = pl.BlockSpec((1, ntg, DYN_ROWS, LANE), lambda b, pt, ix: (b, 0, 0, 0))
    first = pl.BlockSpec((1, 1, SLC_BLOCK, LANE), lambda b, pt, ix: (l, pt[b, 0] * per_page, 0, 0))
    prev = pl.BlockSpec((1, 1, SLC_BLOCK, LANE), lambda b, pt, ix: (l, pt[b, N_PAGES - 1] * per_page + per_page - 1, 0, 0))
    new8 = pl.BlockSpec((1, 8, LANE), per_b)
    win = pl.BlockSpec((1, 1, WIN_BUF, LANE), lambda b, pt, ix: (l, b, 0, 0))
    col32 = pl.BlockSpec((1, NCOL, LANE), per_b)
    gs = pltpu.PrefetchScalarGridSpec(
        num_scalar_prefetch=2, grid=(B,),
        in_specs=[dense, dense, first, prev, first, prev, new8, new8, win, win, new8, new8, col32, col32, col32,
                  pl.BlockSpec((8, LANE), const), pl.BlockSpec((ROWS_ALL, LANE), const)],
        out_specs=col32,
        scratch_shapes=[pltpu.VMEM((ROWS_ALL, LANE), F32)])
    return pl.pallas_call(
        _sample_sel_kernel, grid_spec=gs,
        out_shape=jax.ShapeDtypeStruct((B, NCOL, LANE), F32),
        compiler_params=pltpu.CompilerParams(dimension_semantics=("arbitrary",), vmem_limit_bytes=VMEM_LIMIT),
        name="sample_sel",
    )(page_table, idx, kd, vd, pool_k, pool_k, pool_v, pool_v, knew, vnew, kwin, vwin, kwnew, vwnew,
      q32, gate32, oc, jnp.asarray(info), jnp.asarray(_sel_mask()))


def _old_sample_sel_kernel(pt_ref, idx_ref, *refs):
    nd = DEC_SEQ * KV_GROUPS * N_DYN
    k_dyn, v_dyn = refs[0:nd], refs[nd:2 * nd]
    (k0_ref, k1_ref, v0_ref, v1_ref, knew_ref, vnew_ref, kwb_ref, vwb_ref, kwn_ref, vwn_ref,
     q_ref, gate_ref, oc_ref, info_ref, mask_ref, o_ref, kall, vall, kpos) = refs[2 * nd:]
    b = pl.program_id(0)
    w = _query_cols(q_ref[0])
    slope = info_ref[0:1, :]
    qpos = info_ref[1:2, :]
    tcol = info_ref[2:3, :]
    okc = info_ref[3:4, :]
    r_f = lax.broadcasted_iota(jnp.int32, (NEW_PAD, LANE), 0).astype(F32)
    cur = (PAST_LEN + DEC_SEQ - 1) // SLC_BLOCK

    def put(row0, k_blk, v_blk, first_pos):
        n = k_blk.shape[0]
        kall[row0:row0 + n, :] = k_blk.astype(BF16)
        vall[row0:row0 + n, :] = v_blk.astype(BF16)
        kpos[row0:row0 + n, :] = first_pos + r_f[0:n, :]

    put(0, k0_ref[0, 0], v0_ref[0, 0], 0.0)
    put(SLC_BLOCK, k1_ref[0, 0], v1_ref[0, 0], float((cur - 1) * SLC_BLOCK))
    ntg = DEC_SEQ * KV_GROUPS
    for tg in range(ntg):
        for k in range(N_DYN):
            blk = idx_ref[b, (k + 3) * ntg + tg]
            put(ROW_DYN0 + (tg * N_DYN + k) * SLC_BLOCK, k_dyn[tg * N_DYN + k][0, 0], v_dyn[tg * N_DYN + k][0, 0],
                (blk * SLC_BLOCK).astype(F32))
    zpad = jnp.zeros((NEW_PAD - 8, LANE), F32)
    put(ROW_NEW, jnp.concatenate([knew_ref[0], zpad], axis=0), jnp.concatenate([vnew_ref[0], zpad], axis=0), float(cur * SLC_BLOCK))

    def attend(k_rows, v_rows, mask, dist):
        s = jnp.where(mask, _mm(k_rows, w) - slope * dist, NEG)
        p = jnp.exp(s - jnp.max(s, axis=0, keepdims=True))
        l = jnp.sum(p, axis=0, keepdims=True)
        o_t = lax.dot_general(v_rows, p.astype(BF16), (((0,), (0,)), ((), ())), preferred_element_type=F32)
        return (o_t / l).T[0:NCOL, :]

    o_s = attend(kall[...], vall[...], mask_ref[...] > 0.5, qpos - kpos[...])
    i_w = lax.broadcasted_iota(jnp.int32, (WIN_ROWS, LANE), 0).astype(F32)
    mask_w = (i_w >= tcol) & (i_w <= tcol + float(WIN_BUF)) & (okc > 0.5)
    kw_rows = jnp.concatenate([kwb_ref[0, 0], kwn_ref[0], zpad], axis=0).astype(BF16)
    vw_rows = jnp.concatenate([vwb_ref[0, 0], vwn_ref[0], zpad], axis=0).astype(BF16)
    o_w = attend(kw_rows, vw_rows, mask_w, float(WIN_BUF) + tcol - i_w)
    gates = jax.nn.sigmoid(gate_ref[0])
    o_ref[0] = _own_group_lanes(gates[:, 0:1] * oc_ref[0] + gates[:, 1:2] * o_s + gates[:, 2:3] * o_w)


def sample_sel(l, pool_k, pool_v, page_table, idx, knew, vnew, kwin, vwin, kwnew, vwnew, q32, gate32, oc):
    B = page_table.shape[0]
    info, _, _, _ = _col_consts()
    ntg = DEC_SEQ * KV_GROUPS
    per_page = PAGE_SIZE // SLC_BLOCK

    def dyn(tg, k):
        def im(b, pt, ix):
            blk = jnp.minimum(ix[b, (k + 3) * ntg + tg], S_NSB - 2)
            return (l, pt[b, blk // per_page] * per_page + blk % per_page, 0, 0)
        return pl.BlockSpec((1, 1, SLC_BLOCK, LANE), im)
    first = pl.BlockSpec((1, 1, SLC_BLOCK, LANE), lambda b, pt, ix: (l, pt[b, 0] * per_page, 0, 0))
    prev = pl.BlockSpec((1, 1, SLC_BLOCK, LANE), lambda b, pt, ix: (l, pt[b, N_PAGES - 1] * per_page + per_page - 1, 0, 0))
    per_b = lambda b, pt, ix: (b, 0, 0)
    const = lambda b, pt, ix: (0, 0)
    new8 = pl.BlockSpec((1, 8, LANE), per_b)
    win = pl.BlockSpec((1, 1, WIN_BUF, LANE), lambda b, pt, ix: (l, b, 0, 0))
    col32 = pl.BlockSpec((1, NCOL, LANE), per_b)
    dyn_specs = [dyn(tg, k) for tg in range(ntg) for k in range(N_DYN)]
    nd = len(dyn_specs)
    gs = pltpu.PrefetchScalarGridSpec(
        num_scalar_prefetch=2, grid=(B,),
        in_specs=dyn_specs + dyn_specs + [first, prev, first, prev, new8, new8, win, win, new8, new8, col32, col32, col32,
                                          pl.BlockSpec((8, LANE), const), pl.BlockSpec((ROWS_ALL, LANE), const)],
        out_specs=col32,
        scratch_shapes=[pltpu.VMEM((ROWS_ALL, LANE), BF16), pltpu.VMEM((ROWS_ALL, LANE), BF16), pltpu.VMEM((ROWS_ALL, LANE), F32)])
    return pl.pallas_call(
        _sample_sel_kernel, grid_spec=gs,
        out_shape=jax.ShapeDtypeStruct((B, NCOL, LANE), F32),
        compiler_params=pltpu.CompilerParams(dimension_semantics=("arbitrary",), vmem_limit_bytes=VMEM_LIMIT),
        name="sample_sel",
    )(page_table, idx, *([pool_k] * nd), *([pool_v] * nd), pool_k, pool_k, pool_v, pool_v, knew, vnew, kwin, vwin, kwnew, vwnew,
      q32, gate32, oc, jnp.asarray(info), jnp.asarray(_sel_mask()))


def nsa_sample_attn(l, proj, pools, win, page_table, cw):
    B = page_table.shape[0]
    T = DEC_SEQ
    pool_kc, pool_vc, pool_ks, pool_vs = pools
    n_pool = pool_kc.shape[1]
    pages = lambda p: p.reshape(DEPTH, n_pool, PAGE_SIZE, LANE)
    halves = lambda p: p.reshape(DEPTH, n_pool * (PAGE_SIZE // SLC_BLOCK), SLC_BLOCK, LANE)
    kc_s, vc_st = compress_paged(l, pages(pool_kc), pages(pool_vc), page_table, cw)
    q32 = proj[:, COL_Q:COL_Q + N_HEADS * LANE].reshape(B, NCOL, LANE)
    oc, idx = sample_cmp(q32, kc_s, vc_st)
    idx = idx[:, :, :T * KV_GROUPS].reshape(B, N_SELECT * T * KV_GROUPS)

    def new8(c):
        return jnp.pad(proj[:, c:c + LANE].reshape(B, T, LANE), ((0, 0), (0, 8 - T), (0, 0)))
    small = proj[:, COL_SMALL:COL_SMALL + LANE].reshape(B, T, LANE)
    gate32 = jnp.pad(small[:, :, SMALL_G:SMALL_G + 3 * N_HEADS].reshape(B, NCOL, 3), ((0, 0), (0, 0), (0, LANE - 3)))
    wins = [w.reshape(DEPTH, B, WIN_BUF, LANE) for w in win]
    o32 = sample_sel_dense(l, halves(pool_ks), halves(pool_vs), page_table, idx, new8(COL_KV + 2 * LANE), new8(COL_KV + 3 * LANE),
                     wins[0], wins[1], new8(COL_KV + 4 * LANE), new8(COL_KV + 5 * LANE), q32, gate32, oc)
    o4 = o32.reshape(B, T, N_HEADS, KV_GROUPS, HEAD_DIM)
    o = jnp.concatenate([o4[:, :, :HEADS_PER_GROUP, 0], o4[:, :, HEADS_PER_GROUP:, 1]], axis=2)
    return o.reshape(B * T, D_B)


def alibi_slopes():
    return jnp.asarray(2.0 ** (-8.0 * np.arange(1, N_HEADS + 1) / N_HEADS), jnp.float32)


def masked_softmax(s, mask):
    p = jax.nn.softmax(jnp.where(mask, s, NEG), axis=-1)
    return jnp.where(mask, p, 0.0)


def pad_rows(a, mult):
    pad = (-a.shape[1]) % mult
    return jnp.pad(a, ((0, 0), (0, pad)) + ((0, 0),) * (a.ndim - 2))


def gather_pages(pool, page_table):
    rows = pool[page_table]
    return rows.reshape((page_table.shape[0], -1) + pool.shape[2:])


def causal_conv(u, buf, w, b):
    T = u.shape[1]
    ext = jnp.concatenate([buf.astype(u.dtype), u], axis=1)
    out = b + ext[:, 0:T] * w[0]
    for j in range(1, CONV_W):
        out = out + ext[:, j:j + T] * w[j]
    return out, ext[:, T:]


def mlstm_chunk(carry, inp):
    C, n, m = carry
    q, k, v, ig, lf = inp
    L = q.shape[1]
    Fh = jnp.cumsum(lf, axis=1).transpose(0, 2, 1)
    igh = ig.transpose(0, 2, 1)
    causal = jnp.tril(jnp.ones((L, L), dtype=bool))
    D = jnp.where(causal, Fh[:, :, :, None] - Fh[:, :, None, :] + igh[:, :, None, :], NEG)
    a = Fh + m[:, :, None]
    m_t = jnp.maximum(a, D.max(-1))
    S = jnp.einsum('bthd,bshd->bhts', q, k) * jnp.exp(D - m_t[..., None])
    inter = jnp.exp(a - m_t)
    num = jnp.einsum('bhts,bshd->bthd', S, v) + inter.transpose(0, 2, 1)[..., None] * jnp.einsum('bthk,bhkv->bthv', q, C)
    den = S.sum(-1) + inter * jnp.einsum('bthk,bhk->bht', q, n)
    denom = jnp.maximum(jnp.abs(den), jnp.exp(-m_t)).transpose(0, 2, 1)[..., None]
    h = num / denom
    F_L = Fh[:, :, -1]
    w_s = F_L[:, :, None] - Fh + igh
    m_new = jnp.maximum(F_L + m, w_s.max(-1))
    decay = jnp.exp(F_L + m - m_new)
    ws = jnp.exp(w_s - m_new[..., None])
    C_new = decay[..., None, None] * C + jnp.einsum('bhs,bshk,bshv->bhkv', ws, k, v)
    n_new = decay[..., None] * n + jnp.einsum('bhs,bshk->bhk', ws, k)
    return (C_new, n_new, m_new), h


def mlstm_branch(qk_pre, v_pre, o_pre, i_pre, f_pre, conv_buf, C, n, m, conv_w, conv_b, head_g):
    f32 = jnp.float32
    B, T = qk_pre.shape[:2]
    qk, conv_new = causal_conv(qk_pre, conv_buf, conv_w, conv_b)
    qk = jax.nn.silu(qk.astype(f32))
    q = qk[..., :D_A].reshape(B, T, M_HEADS, M_HEAD_DIM)
    k = qk[..., D_A:].reshape(B, T, M_HEADS, M_HEAD_DIM) * (M_HEAD_DIM ** -0.5)
    v = v_pre.astype(f32).reshape(B, T, M_HEADS, M_HEAD_DIM)
    ig = i_pre.astype(f32)
    lf = jax.nn.log_sigmoid(f_pre.astype(f32))
    L = M_CHUNK if T % M_CHUNK == 0 else T
    nc = T // L

    def chunks(a):
        return jnp.swapaxes(a.reshape((B, nc, L) + a.shape[2:]), 0, 1)

    (C1, n1, m1), h = lax.scan(mlstm_chunk, (C, n, m), (chunks(q), chunks(k), chunks(v), chunks(ig), chunks(lf)))
    h = jnp.swapaxes(h, 0, 1).reshape(B, T, M_HEADS, M_HEAD_DIM)
    mu = jnp.mean(h, axis=-1, keepdims=True)
    var = jnp.mean(jnp.square(h - mu), axis=-1, keepdims=True)
    hn = ((h - mu) * lax.rsqrt(var + RMS_EPS)).reshape(B, T, D_A) * head_g.astype(f32)
    out = jax.nn.sigmoid(o_pre.astype(f32)) * hn
    return out.astype(qk_pre.dtype), conv_new, C1, n1, m1


def compress(rows, w_pos, w_proj):
    B, Lp = rows.shape[:2]
    sub = rows.reshape(B, Lp // CMP_STRIDE, CMP_STRIDE, KV_GROUPS, HEAD_DIM)
    head = jnp.einsum('bnjgd,jd->bngd', sub, w_pos[:CMP_STRIDE])
    tail = jnp.einsum('bnjgd,jd->bngd', sub, w_pos[CMP_STRIDE:])
    pooled = head[:, :-1] + tail[:, 1:]
    return jnp.einsum('bngd,de->bnge', pooled, w_proj)


def to_blocks(a):
    B, Lp = a.shape[:2]
    return a.reshape(B, Lp // SLC_BLOCK, SLC_BLOCK, KV_GROUPS, HEAD_DIM).transpose(0, 3, 1, 2, 4)


def nsa_attend(q, qpos, g, kc, vc, c_end, ksb, vsb, kw, vw, wpos, slopes):
    f32 = jnp.float32
    B, T = q.shape[:2]
    G, I, dh = KV_GROUPS, HEADS_PER_GROUP, HEAD_DIM
    scale = dh ** -0.5
    qg = q.reshape(B, T, G, I, dh)
    gg = g.reshape(B, T, G, I, 3)
    sl = slopes.reshape(G, I)[None, None, :, :, None]
    tq = qpos.astype(f32)
    dist_c = tq[:, None] - c_end[None, :].astype(f32)
    mask_c = (c_end[None, :] <= qpos[:, None])[None, :, None, None, :]
    s_c = jnp.einsum('btgid,bngd->btgin', qg, kc).astype(f32) * scale - sl * dist_c[None, :, None, None, :]
    p_c = masked_softmax(s_c, mask_c)
    o_c = jnp.einsum('btgin,bngd->btgid', p_c.astype(vc.dtype), vc)
    NC = kc.shape[1]
    NS = ksb.shape[2]
    c_start = jnp.arange(NC, dtype=jnp.int32) * CMP_STRIDE
    s_start = jnp.arange(NS, dtype=jnp.int32) * SLC_BLOCK
    overlap = ((c_start[:, None] < s_start[None, :] + SLC_BLOCK) & (c_start[:, None] + CMP_BLOCK > s_start[None, :])).astype(f32)
    imp = jnp.einsum('btgin,nj->btgj', p_c, overlap)
    blk = jnp.arange(NS, dtype=jnp.int32)[None, :]
    cur = (qpos // SLC_BLOCK)[:, None]
    forced = (blk == 0) | (blk == cur) | (blk == cur - 1)
    valid = s_start[None, :] <= qpos[:, None]
    score = jnp.where(forced[None, :, None, :], BIG, jnp.where(valid[None, :, None, :], imp, NEG))
    n_sel = min(N_SELECT, NS)
    _, idx = lax.top_k(score, n_sel)
    idx_g = idx.transpose(0, 2, 1, 3).reshape(B, G, T * n_sel)
    gather = jax.vmap(jax.vmap(lambda a, i: a[i]))
    kg = gather(ksb, idx_g).reshape(B, G, T, n_sel, SLC_BLOCK, dh)
    vg = gather(vsb, idx_g).reshape(B, G, T, n_sel, SLC_BLOCK, dh)
    pos_s = idx[..., None] * SLC_BLOCK + jnp.arange(SLC_BLOCK, dtype=jnp.int32)
    dist_s = tq[None, :, None, None, None] - pos_s.astype(f32)
    mask_s = (pos_s <= qpos[None, :, None, None, None]).reshape(B, T, G, 1, n_sel * SLC_BLOCK)
    s_s = jnp.einsum('btgid,bgtnpd->btginp', qg, kg).astype(f32) * scale - sl[..., None] * dist_s[:, :, :, None]
    p_s = masked_softmax(s_s.reshape(B, T, G, I, n_sel * SLC_BLOCK), mask_s).reshape(B, T, G, I, n_sel, SLC_BLOCK)
    o_s = jnp.einsum('btginp,bgtnpd->btgid', p_s.astype(vg.dtype), vg)
    delta = qpos[:, None] - wpos[None, :]
    mask_w = ((delta >= 0) & (delta <= WINDOW) & (wpos[None, :] >= 0))[None, :, None, None, :]
    s_w = jnp.einsum('btgid,bwgd->btgiw', qg, kw).astype(f32) * scale - sl * delta.astype(f32)[None, :, None, None, :]
    p_w = masked_softmax(s_w, mask_w)
    o_w = jnp.einsum('btgiw,bwgd->btgid', p_w.astype(vw.dtype), vw)
    o = gg[..., 0:1] * o_c + gg[..., 1:2] * o_s + gg[..., 2:3] * o_w
    return o.reshape(B, T, G * I * dh).astype(q.dtype)


def nsa_prompt(q, g, kc, vc, ks, vs, kw, vw, cmp, slopes):
    B, S = q.shape[:2]
    pos_k, pos_v, proj_k, proj_v = cmp
    kc_s = compress(pad_rows(kc, SLC_BLOCK), pos_k, proj_k)
    vc_s = compress(pad_rows(vc, SLC_BLOCK), pos_v, proj_v)
    c_end = jnp.arange(kc_s.shape[1], dtype=jnp.int32) * CMP_STRIDE + (CMP_BLOCK - 1)
    ksb = to_blocks(pad_rows(ks, SLC_BLOCK))
    vsb = to_blocks(pad_rows(vs, SLC_BLOCK))
    kw_pad = jnp.pad(kw, ((0, 0), (WINDOW, 0), (0, 0), (0, 0)))
    vw_pad = jnp.pad(vw, ((0, 0), (WINDOW, 0), (0, 0), (0, 0)))

    def one_block(start):
        qb = lax.dynamic_slice_in_dim(q, start, Q_BLOCK, axis=1)
        gb = lax.dynamic_slice_in_dim(g, start, Q_BLOCK, axis=1)
        kwb = lax.dynamic_slice_in_dim(kw_pad, start, WINDOW + Q_BLOCK, axis=1)
        vwb = lax.dynamic_slice_in_dim(vw_pad, start, WINDOW + Q_BLOCK, axis=1)
        qpos = start + jnp.arange(Q_BLOCK, dtype=jnp.int32)
        wpos = start - WINDOW + jnp.arange(WINDOW + Q_BLOCK, dtype=jnp.int32)
        return nsa_attend(qb, qpos, gb, kc_s, vc_s, c_end, ksb, vsb, kwb, vwb, wpos, slopes)

    starts = jnp.arange(S // Q_BLOCK, dtype=jnp.int32) * Q_BLOCK
    o = lax.map(one_block, starts)
    o = jnp.moveaxis(o, 0, 1).reshape(B, S, D_B)
    return o


def nsa_sample(q, g, kc, vc, ks, vs, kw, vw, cmp, slopes, pools, win, page_table):
    T = q.shape[1]
    pos_k, pos_v, proj_k, proj_v = cmp
    pool_kc, pool_vc, pool_ks, pool_vs = pools
    buf_kw, buf_vw = win
    past_len = page_table.shape[1] * PAGE_SIZE

    def full(pool, new):
        past = gather_pages(pool, page_table).astype(new.dtype)
        return pad_rows(jnp.concatenate([past, new], axis=1), SLC_BLOCK)

    kc_s = compress(full(pool_kc, kc), pos_k, proj_k)
    vc_s = compress(full(pool_vc, vc), pos_v, proj_v)
    c_end = jnp.arange(kc_s.shape[1], dtype=jnp.int32) * CMP_STRIDE + (CMP_BLOCK - 1)
    ksb = to_blocks(full(pool_ks, ks))
    vsb = to_blocks(full(pool_vs, vs))
    kw_all = jnp.concatenate([buf_kw.astype(kw.dtype), kw], axis=1)
    vw_all = jnp.concatenate([buf_vw.astype(vw.dtype), vw], axis=1)
    wbuf = buf_kw.shape[1]
    wpos = past_len - wbuf + jnp.arange(wbuf + T, dtype=jnp.int32)
    qpos = past_len + jnp.arange(T, dtype=jnp.int32)
    o = nsa_attend(q, qpos, g, kc_s, vc_s, c_end, ksb, vsb, kw_all, vw_all, wpos, slopes)
    return o


def _layer(x, B, T, lw, state, cmp, sample_ctx, g_final, final_norm):
    proj = rms_proj(x, lw["g_mix"], lw["w_in"], lw["b_in"])

    def piece(c0, width):
        return proj[:, c0:c0 + width].reshape(B, T, width)

    m_qk, m_v, m_o = piece(COL_QK, 2 * D_A), piece(COL_V, D_A), piece(COL_O, D_A)
    small = piece(COL_SMALL, LANE)
    kv = [piece(COL_KV + i * KV_W, KV_W).reshape(B, T, KV_GROUPS, HEAD_DIM) for i in range(6)]
    kc, vc, ks, vs, kw, vw = kv

    conv_buf, C, n, m = state
    conv_new = jnp.concatenate([conv_buf, m_qk], axis=1)[:, T:]
    margs = (conv_buf, C, n, m, lw["conv_w"], lw["conv_b"], lw["m_norm"])
    if sample_ctx is None:
        h_a, C1, n1, m1 = mlstm(B, T, proj, COL_QK // (2 * D_A), proj, COL_V // D_A, proj, COL_O // D_A, proj, COL_SMALL // LANE, *margs)
    else:
        def pad_tok(a, fill):
            return jnp.concatenate([a, jnp.broadcast_to(fill, (B, M_CHUNK - T, a.shape[-1]))], axis=1).reshape(B * M_CHUNK, a.shape[-1])
        fill_small = jnp.zeros((LANE,), F32).at[SMALL_I:SMALL_I + M_HEADS].set(NEG).at[SMALL_F:SMALL_F + M_HEADS].set(BIG)
        zero = jnp.zeros((1,), F32)
        h_pad, C1, n1, m1 = mlstm(B, M_CHUNK, pad_tok(m_qk, zero), 0, pad_tok(m_v, zero), 0, pad_tok(m_o, zero), 0, pad_tok(small, fill_small), 0, *margs)
        h_a = h_pad.reshape(B, M_CHUNK, D_A)[:, :T].reshape(B * T, D_A)
    cw = _cmp_weights(*cmp)
    if sample_ctx is None:
        kvc = COL_KV // LANE
        kc_s, vc_st = compress_prompt(B, T, proj, kvc, proj, kvc + 1, cw)
        o_b = nsa_prompt_attn(B, T, proj, kc_s, vc_st)
        keep = min(WINDOW, T)
        kw_state, vw_state = kw[:, T - keep:], vw[:, T - keep:]
    else:
        l, pools, win, page_table = sample_ctx
        o_b = nsa_sample_attn(l, proj, pools, win, page_table, cw)
        keep = min(WINDOW, WIN_BUF + T)
        kw_state = jnp.concatenate([win[0][l], kw], axis=1)[:, WIN_BUF + T - keep:]
        vw_state = jnp.concatenate([win[1][l], vw], axis=1)[:, WIN_BUF + T - keep:]

    x = merge(x, h_a.reshape(B * T, D_A), o_b.reshape(B * T, D_B), proj, lw["w_a"], lw["w_b"], lw["w_o"])
    x = ffn(x, lw["g_ffn"], lw["w_ffn_in"], lw["w_ffn_out"], g_final, final_norm)
    return x, (kc, vc, ks, vs, kw_state, vw_state, conv_new, C1, n1, m1)


def kernel(x_prompt, x_sample, cache_k_cmp, cache_v_cmp, cache_k_slc, cache_v_slc, state_k_win, state_v_win, state_conv, state_C, state_n, state_m, page_table, norm_mix, w_in, b_in, conv_w, conv_b, cmp_pos_k, cmp_pos_v, cmp_proj_k, cmp_proj_v, m_norm, w_a, w_b, w_o, norm_ffn, w_ffn_in, w_ffn_out, norm_final):
    yp = x_prompt.reshape(BATCH * SEQ, D_MODEL)
    ys = x_sample.reshape(DEC_BATCH * DEC_SEQ, D_MODEL)
    p_states, s_states = [], []
    for l in range(DEPTH):
        w_in_l, b_in_l = _prep_w_in(w_in[l], b_in[l])
        lw = dict(g_mix=norm_mix[l], w_in=w_in_l, b_in=b_in_l, conv_w=conv_w[l], conv_b=conv_b[l], m_norm=m_norm[l],
                  w_a=w_a[l].astype(BF16), w_b=w_b[l].astype(BF16), w_o=w_o[l].astype(BF16), g_ffn=norm_ffn[l],
                  w_ffn_in=w_ffn_in[l].astype(BF16), w_ffn_out=w_ffn_out[l].astype(BF16))
        cmp = (cmp_pos_k[l], cmp_pos_v[l], cmp_proj_k[l], cmp_proj_v[l])
        last = l == DEPTH - 1
        conv0 = jnp.zeros((BATCH, CONV_W - 1, 2 * D_A), F32)
        C0 = jnp.zeros((BATCH, M_HEADS, M_HEAD_DIM, M_HEAD_DIM), F32)
        n0 = jnp.zeros((BATCH, M_HEADS, M_HEAD_DIM), F32)
        m0 = jnp.full((BATCH, M_HEADS), NEG, F32)
        yp, st_p = _layer(yp, BATCH, SEQ, lw, (conv0, C0, n0, m0), cmp, None, norm_final, last)
        p_states.append(st_p)
        ctx = (l, (cache_k_cmp, cache_v_cmp, cache_k_slc, cache_v_slc), (state_k_win, state_v_win), page_table)
        ys, st_s = _layer(ys, DEC_BATCH, DEC_SEQ, lw, (state_conv[l], state_C[l], state_n[l], state_m[l]), cmp, ctx, norm_final, last)
        s_states.append(st_s)
    y_prompt = yp.reshape(BATCH, SEQ, D_MODEL)
    y_sample = ys.reshape(DEC_BATCH, DEC_SEQ, D_MODEL)
    P = [jnp.stack(a) for a in zip(*p_states)]
    S = [jnp.stack(a) for a in zip(*s_states)]
    return (y_prompt, y_sample, P[0], S[0], P[1], S[1], P[2], S[2], P[3], S[3], P[4], S[4],
            P[5], S[5], P[6], S[6], P[7], S[7], P[8], S[8], P[9], S[9])
```

```python
import functools
import jax, jax.numpy as jnp
from jax import lax
import numpy as np
from jax.experimental import pallas as pl
from jax.experimental.pallas import tpu as pltpu

D_MODEL = 1024
BATCH = 2
SEQ = 8192
DEPTH = 2
DEC_BATCH = 32
DEC_SEQ = 4
PAST_LEN = 16384
PAGE_SIZE = 128
M_HEADS = 4
M_HEAD_DIM = D_MODEL // 8
D_A = M_HEADS * M_HEAD_DIM
CONV_W = 4
M_CHUNK = 128
N_HEADS = 8
HEAD_DIM = D_MODEL // 16
D_B = N_HEADS * HEAD_DIM
KV_GROUPS = 2
HEADS_PER_GROUP = N_HEADS // KV_GROUPS
KV_W = KV_GROUPS * HEAD_DIM
CMP_STRIDE = 16
CMP_BLOCK = 2 * CMP_STRIDE
SLC_BLOCK = 64
N_SELECT = 16
WINDOW = 512
Q_BLOCK = 128
D_FF = -(-8 * D_MODEL // (3 * 256)) * 256
RMS_EPS = 1e-6
NEG = -1e30
BIG = 1e30
SIZES = (2 * D_A, D_A, D_A, M_HEADS, M_HEADS, D_B, KV_W, KV_W, KV_W, KV_W, KV_W, KV_W, 3 * N_HEADS, 2 * D_MODEL)
SPLITS = tuple(int(s) for s in np.cumsum(SIZES)[:-1])
D_IN = sum(SIZES)

F32 = jnp.float32
BF16 = jnp.bfloat16
LANE = 128
VMEM_LIMIT = 48 * 1024 * 1024

COL_QK = 0
COL_GMA = 1024
COL_GMB = 2048
COL_V = 3072
COL_O = 3584
COL_Q = 4096
COL_KV = 5120
COL_SMALL = 5888
N_PROJ = 6144
SMALL_I = 0
SMALL_F = M_HEADS
SMALL_G = 2 * M_HEADS


def _prep_w_in(w, b):
    def cols(a):
        (m_qk, m_v, m_o, m_i, m_f, a_q, a_kc, a_vc, a_ks, a_vs, a_kw, a_vw, a_g, g_mrg) = jnp.split(a, SPLITS, axis=-1)
        z64 = jnp.zeros(a.shape[:-1] + (HEAD_DIM,), a.dtype)
        slabs = []
        for h in range(N_HEADS):
            qh = a_q[..., h * HEAD_DIM:(h + 1) * HEAD_DIM]
            slabs += [qh, z64] if h < HEADS_PER_GROUP else [z64, qh]
        small = jnp.concatenate([m_i, m_f, a_g, jnp.zeros(a.shape[:-1] + (LANE - 2 * M_HEADS - 3 * N_HEADS,), a.dtype)], -1)
        pad = jnp.zeros(a.shape[:-1] + (N_PROJ - COL_SMALL - LANE,), a.dtype)
        return jnp.concatenate([m_qk, g_mrg, m_v, m_o] + slabs + [a_kc, a_vc, a_ks, a_vs, a_kw, a_vw, small, pad], -1)
    return cols(w).astype(BF16), cols(b[None, :])


def _rms_proj_kernel(x_ref, g_ref, w_ref, b_ref, o_ref, h_scr):
    @pl.when(pl.program_id(1) == 0)
    def _():
        x = x_ref[...]
        ms = jnp.mean(x * x, axis=-1, keepdims=True)
        h_scr[...] = (x * lax.rsqrt(ms + RMS_EPS) * g_ref[...]).astype(BF16)
    o_ref[...] = jnp.dot(h_scr[...], w_ref[...], preferred_element_type=F32) + b_ref[...]


def rms_proj(x, g, w, b):
    T, D = x.shape
    N = w.shape[1]
    tm = min(T, 1024)
    tn = 512
    return pl.pallas_call(
        _rms_proj_kernel,
        grid=(T // tm, N // tn),
        in_specs=[pl.BlockSpec((tm, D), lambda i, j: (i, 0)),
                  pl.BlockSpec((1, D), lambda i, j: (0, 0)),
                  pl.BlockSpec((D, tn), lambda i, j: (0, j)),
                  pl.BlockSpec((1, tn), lambda i, j: (0, j))],
        out_specs=pl.BlockSpec((tm, tn), lambda i, j: (i, j)),
        out_shape=jax.ShapeDtypeStruct((T, N), F32),
        scratch_shapes=[pltpu.VMEM((tm, D), BF16)],
        compiler_params=pltpu.CompilerParams(dimension_semantics=("arbitrary", "arbitrary"), vmem_limit_bytes=VMEM_LIMIT),
        name="rms_proj",
    )(x, g.reshape(1, D), w, b)


def _merge_kernel(x_ref, ha_ref, ob_ref, ga_ref, gb_ref, wa_ref, wb_ref, wo_ref, o_ref):
    ya = jnp.dot(ha_ref[...].astype(BF16), wa_ref[...], preferred_element_type=F32)
    yb = jnp.dot(ob_ref[...].astype(BF16), wb_ref[...], preferred_element_type=F32)
    y = jax.nn.sigmoid(ga_ref[...]) * ya + jax.nn.sigmoid(gb_ref[...]) * yb
    o_ref[...] = x_ref[...] + jnp.dot(y.astype(BF16), wo_ref[...], preferred_element_type=F32)


def merge(x, ha, ob, proj, wa, wb, wo):
    T, D = x.shape
    tm = min(T, 512)
    row = lambda i: (i, 0)
    const = lambda i: (0, 0)
    return pl.pallas_call(
        _merge_kernel,
        grid=(T // tm,),
        in_specs=[pl.BlockSpec((tm, D), row),
                  pl.BlockSpec((tm, D_A), row),
                  pl.BlockSpec((tm, D_B), row),
                  pl.BlockSpec((tm, D), lambda i: (i, COL_GMA // D_MODEL)),
                  pl.BlockSpec((tm, D), lambda i: (i, COL_GMB // D_MODEL)),
                  pl.BlockSpec((D_A, D), const),
                  pl.BlockSpec((D_B, D), const),
                  pl.BlockSpec((D, D), const)],
        out_specs=pl.BlockSpec((tm, D), row),
        out_shape=jax.ShapeDtypeStruct((T, D), F32),
        compiler_params=pltpu.CompilerParams(dimension_semantics=("arbitrary",), vmem_limit_bytes=VMEM_LIMIT),
        name="merge",
    )(x, ha, ob, proj, proj, wa, wb, wo)


def _ffn_kernel(x_ref, g_ref, wg_ref, wu_ref, wo_ref, gf_ref, o_ref, h_scr, acc_scr, *, final_norm):
    j = pl.program_id(1)

    @pl.when(j == 0)
    def _():
        x = x_ref[...]
        ms = jnp.mean(x * x, axis=-1, keepdims=True)
        h_scr[...] = (x * lax.rsqrt(ms + RMS_EPS) * g_ref[...]).astype(BF16)
        acc_scr[...] = jnp.zeros_like(acc_scr)

    h = h_scr[...]
    gate = jnp.dot(h, wg_ref[...], preferred_element_type=F32)
    up = jnp.dot(h, wu_ref[...], preferred_element_type=F32)
    act = (gate * jax.nn.sigmoid(gate) * up).astype(BF16)
    acc_scr[...] += jnp.dot(act, wo_ref[...], preferred_element_type=F32)

    @pl.when(j == pl.num_programs(1) - 1)
    def _():
        y = x_ref[...] + acc_scr[...]
        if final_norm:
            ms = jnp.mean(y * y, axis=-1, keepdims=True)
            y = y * lax.rsqrt(ms + RMS_EPS) * gf_ref[...]
        o_ref[...] = y


def ffn(x, g, w_in, w_out, g_final, final_norm):
    T, D = x.shape
    tm = min(T, 1024)
    tf = 256
    nf = D_FF // tf
    return pl.pallas_call(
        functools.partial(_ffn_kernel, final_norm=final_norm),
        grid=(T // tm, nf),
        in_specs=[pl.BlockSpec((tm, D), lambda i, j: (i, 0)),
                  pl.BlockSpec((1, D), lambda i, j: (0, 0)),
                  pl.BlockSpec((D, tf), lambda i, j: (0, j)),
                  pl.BlockSpec((D, tf), lambda i, j: (0, j + nf)),
                  pl.BlockSpec((tf, D), lambda i, j: (j, 0)),
                  pl.BlockSpec((1, D), lambda i, j: (0, 0))],
        out_specs=pl.BlockSpec((tm, D), lambda i, j: (i, 0)),
        out_shape=jax.ShapeDtypeStruct((T, D), F32),
        scratch_shapes=[pltpu.VMEM((tm, D), BF16), pltpu.VMEM((tm, D), F32)],
        compiler_params=pltpu.CompilerParams(dimension_semantics=("arbitrary", "arbitrary"), vmem_limit_bytes=VMEM_LIMIT),
        name="ffn",
    )(x, g.reshape(1, D), w_in, w_in, w_out, g_final.reshape(1, D))


HI = lax.Precision.HIGHEST


def _mm(a, b):
    return jnp.dot(a, b, preferred_element_type=F32)


def _log_sigmoid(x):
    return jnp.minimum(x, 0.0) - jnp.log1p(jnp.exp(-jnp.abs(x)))


def _mlstm_kernel(qk_ref, v_ref, o_ref, small_ref, cbuf_ref, c0_ref, n0_ref, m0_ref, cw_ref, cb_ref, g_ref,
                  h_ref, c_out_ref, n_out_ref, m_out_ref, ext_scr, c_scr, n_scr, m_scr):
    L = M_CHUNK
    c = pl.program_id(1)

    @pl.when(c == 0)
    def _():
        ext_scr[0:8, :] = cbuf_ref[0]
        c_scr[...] = c0_ref[0]
        n_scr[...] = n0_ref[0]
        m_scr[...] = m0_ref[0]

    ext_scr[8:8 + L, :] = qk_ref[...]
    conv = cb_ref[...] + ext_scr[5:5 + L, :] * cw_ref[0:1, :]
    for j in range(1, CONV_W):
        conv = conv + ext_scr[5 + j:5 + j + L, :] * cw_ref[j:j + 1, :]
    ext_scr[0:8, :] = qk_ref[L - 8:L, :]
    qk = conv * jax.nn.sigmoid(conv)

    small = small_ref[...]
    small_t = small.T
    row = lax.broadcasted_iota(jnp.int32, (L, L), 0)
    col = lax.broadcasted_iota(jnp.int32, (L, L), 1)
    causal = col <= row
    fcol_all = jnp.dot(causal.astype(F32), _log_sigmoid(small), preferred_element_type=F32, precision=HI)
    frow_all = jnp.dot(_log_sigmoid(small_t), (row <= col).astype(F32), preferred_element_type=F32, precision=HI)

    for h in range(M_HEADS):
        sl = slice(h * M_HEAD_DIM, (h + 1) * M_HEAD_DIM)
        q = qk[:, sl]
        k = qk[:, D_A + h * M_HEAD_DIM:D_A + (h + 1) * M_HEAD_DIM] * (M_HEAD_DIM ** -0.5)
        fc = fcol_all[:, SMALL_F + h:SMALL_F + h + 1]
        fr = frow_all[SMALL_F + h:SMALL_F + h + 1, :]
        igr = small_t[SMALL_I + h:SMALL_I + h + 1, :]
        igc = small[:, SMALL_I + h:SMALL_I + h + 1]
        m_prev = m_scr[h:h + 1, 0:1]
        d = jnp.where(causal, fc - fr + igr, NEG)
        a = fc + m_prev
        m_t = jnp.maximum(a, jnp.max(d, axis=-1, keepdims=True))
        qb = q.astype(BF16)
        kb = k.astype(BF16)
        vb = v_ref[:, sl].astype(BF16)
        s = lax.dot_general(qb, kb, (((1,), (1,)), ((), ())), preferred_element_type=F32) * jnp.exp(d - m_t)
        inter = jnp.exp(a - m_t)
        num = _mm(s.astype(BF16), vb) + inter * _mm(qb, c_scr[h].astype(BF16))
        den = jnp.sum(s, axis=-1, keepdims=True) + inter * jnp.sum(q * n_scr[h:h + 1, :], axis=-1, keepdims=True)
        hh = num / jnp.maximum(jnp.abs(den), jnp.exp(-m_t))
        mu = jnp.mean(hh, axis=-1, keepdims=True)
        var = jnp.mean(jnp.square(hh - mu), axis=-1, keepdims=True)
        hn = (hh - mu) * lax.rsqrt(var + RMS_EPS) * g_ref[:, sl]
        h_ref[:, sl] = jax.nn.sigmoid(o_ref[:, sl]) * hn
        f_l = fr[:, L - 1:L]
        m_new = jnp.maximum(f_l + m_prev, jnp.max(f_l - fr + igr, axis=-1, keepdims=True))
        decay = jnp.exp(f_l + m_prev - m_new)
        ks = k * jnp.exp(f_l - fc + igc - m_new)
        c_scr[h] = decay * c_scr[h] + lax.dot_general(ks.astype(BF16), vb, (((0,), (0,)), ((), ())), preferred_element_type=F32)
        n_scr[h:h + 1, :] = decay * n_scr[h:h + 1, :] + jnp.sum(ks, axis=0, keepdims=True)
        m_scr[h:h + 1, :] = jnp.broadcast_to(m_new, (1, LANE))

    @pl.when(c == pl.num_programs(1) - 1)
    def _():
        c_out_ref[0] = c_scr[...]
        n_out_ref[0] = n_scr[...]
        m_out_ref[0] = m_scr[...]


def mlstm(B, T, qk, qk_col, v, v_col, o, o_col, small, small_col, conv_buf, C0, n0, m0, conv_w, conv_b, head_g):
    L = M_CHUNK
    nch = T // L
    cbuf = jnp.concatenate([jnp.zeros((B, 8 - (CONV_W - 1), 2 * D_A), F32), conv_buf], axis=1)
    n0p = jnp.concatenate([n0, jnp.zeros((B, 8 - M_HEADS, M_HEAD_DIM), F32)], axis=1)
    m0p = jnp.broadcast_to(jnp.concatenate([m0, jnp.zeros((B, 8 - M_HEADS), F32)], axis=1)[:, :, None], (B, 8, LANE))
    tok = lambda cb: (lambda b, c: (b * nch + c, cb))
    per_b3 = lambda b, c: (b, 0, 0)
    per_b4 = lambda b, c: (b, 0, 0, 0)
    const = lambda b, c: (0, 0)
    h, C1, n1, m1 = pl.pallas_call(
        _mlstm_kernel,
        grid=(B, nch),
        in_specs=[pl.BlockSpec((L, 2 * D_A), tok(qk_col)),
                  pl.BlockSpec((L, D_A), tok(v_col)),
                  pl.BlockSpec((L, D_A), tok(o_col)),
                  pl.BlockSpec((L, LANE), tok(small_col)),
                  pl.BlockSpec((1, 8, 2 * D_A), per_b3),
                  pl.BlockSpec((1, M_HEADS, M_HEAD_DIM, M_HEAD_DIM), per_b4),
                  pl.BlockSpec((1, 8, M_HEAD_DIM), per_b3),
                  pl.BlockSpec((1, 8, LANE), per_b3),
                  pl.BlockSpec((CONV_W, 2 * D_A), const),
                  pl.BlockSpec((1, 2 * D_A), const),
                  pl.BlockSpec((1, D_A), const)],
        out_specs=[pl.BlockSpec((L, D_A), tok(0)),
                   pl.BlockSpec((1, M_HEADS, M_HEAD_DIM, M_HEAD_DIM), per_b4),
                   pl.BlockSpec((1, 8, M_HEAD_DIM), per_b3),
                   pl.BlockSpec((1, 8, LANE), per_b3)],
        out_shape=[jax.ShapeDtypeStruct((B * T, D_A), F32),
                   jax.ShapeDtypeStruct((B, M_HEADS, M_HEAD_DIM, M_HEAD_DIM), F32),
                   jax.ShapeDtypeStruct((B, 8, M_HEAD_DIM), F32),
                   jax.ShapeDtypeStruct((B, 8, LANE), F32)],
        scratch_shapes=[pltpu.VMEM((8 + L, 2 * D_A), F32),
                        pltpu.VMEM((M_HEADS, M_HEAD_DIM, M_HEAD_DIM), F32),
                        pltpu.VMEM((8, M_HEAD_DIM), F32),
                        pltpu.VMEM((8, LANE), F32)],
        compiler_params=pltpu.CompilerParams(dimension_semantics=("arbitrary", "arbitrary"), vmem_limit_bytes=VMEM_LIMIT),
        name="mlstm",
    )(qk, v, o, small, cbuf, C0, n0p, m0p, conv_w, conv_b.reshape(1, -1), head_g.reshape(1, -1))
    return h, C1, n1[:, :M_HEADS], m1[:, :M_HEADS, 0]


SLOPES = tuple(float(2.0 ** (-8.0 * (h + 1) / N_HEADS)) for h in range(N_HEADS))
Q_SCALE = HEAD_DIM ** -0.5
REMOVED = -3.0e38


def _pool(x, w_ref, n):
    x3 = x.reshape(n, CMP_STRIDE, LANE)
    head = jnp.sum(x3 * w_ref[0:CMP_STRIDE, :][None], axis=1)
    tail = jnp.sum(x3 * w_ref[CMP_STRIDE:CMP_BLOCK, :][None], axis=1)
    nxt = pltpu.roll(tail, n - 1, 0)
    last = lax.broadcasted_iota(jnp.int32, (n, LANE), 0) == n - 1
    return head + jnp.where(last, 0.0, nxt)


def _compress_kernel(k_ref, v_ref, wk_ref, wv_ref, pk_ref, pv_ref, kc_ref, vct_ref, *, n):
    pk = _pool(k_ref[...], wk_ref, n)
    pv = _pool(v_ref[...], wv_ref, n)
    kc_ref[0] = _mm(pk.astype(BF16), pk_ref[...]).astype(BF16)
    vct_ref[0] = _mm(pv.astype(BF16), pv_ref[...]).T.astype(BF16)


def _cmp_weights(pos_k, pos_v, proj_k, proj_v):
    tile = lambda w: jnp.concatenate([w] * KV_GROUPS, axis=1)
    bd = lambda w: jnp.kron(jnp.eye(KV_GROUPS, dtype=F32), w).astype(BF16)
    return tile(pos_k), tile(pos_v), bd(proj_k), bd(proj_v)


def compress_prompt(B, S, kc, kc_col, vc, vc_col, cw):
    n = S // CMP_STRIDE
    wk, wv, pk, pv = cw
    const = lambda b: (0, 0)
    return pl.pallas_call(
        functools.partial(_compress_kernel, n=n),
        grid=(B,),
        in_specs=[pl.BlockSpec((S, LANE), lambda b: (b, kc_col)),
                  pl.BlockSpec((S, LANE), lambda b: (b, vc_col)),
                  pl.BlockSpec((CMP_BLOCK, LANE), const), pl.BlockSpec((CMP_BLOCK, LANE), const),
                  pl.BlockSpec((LANE, LANE), const), pl.BlockSpec((LANE, LANE), const)],
        out_specs=[pl.BlockSpec((1, n, LANE), lambda b: (b, 0, 0)), pl.BlockSpec((1, LANE, n), lambda b: (b, 0, 0))],
        out_shape=[jax.ShapeDtypeStruct((B, n, LANE), BF16), jax.ShapeDtypeStruct((B, LANE, n), BF16)],
        compiler_params=pltpu.CompilerParams(dimension_semantics=("arbitrary",), vmem_limit_bytes=VMEM_LIMIT),
        name="compress_prompt",
    )(kc, vc, wk, wv, pk, pv)


def _nsa_prompt_kernel(q_ref, small_ref, kc_ref, vct_ref, ks_ref, vs_ref, kw_ref, vw_ref, o_ref,
                       ksb, vstb, kwb, vwtb, qt_scr, sd_scr, p_scr, sel_scr, m_scr, l_scr, acc_scr, oc_scr, *, S):
    QB = Q_BLOCK
    nkb = S // QB
    ncb = S // CMP_STRIDE
    nsb = S // SLC_BLOCK
    G, I, HD = KV_GROUPS, HEADS_PER_GROUP, HEAD_DIM
    qb = pl.program_id(1)
    c_i = lax.broadcasted_iota(jnp.int32, (QB, QB), 0)
    r_i = lax.broadcasted_iota(jnp.int32, (QB, QB), 1)

    @pl.when(qb == 0)
    def _():
        def cp(i, carry):
            rows = pl.ds(pl.multiple_of(i * QB, QB), QB)
            ksb[i] = ks_ref[rows, :].astype(BF16)
            kwb[i] = kw_ref[rows, :].astype(BF16)
            vstb[i] = vs_ref[rows, :].T.astype(BF16)
            vwtb[i] = vw_ref[rows, :].T.astype(BF16)
            return carry
        lax.fori_loop(0, nkb, cp, 0)
        rc = (r_i - c_i).astype(F32)
        for h in range(N_HEADS):
            sd_scr[h] = SLOPES[h] * rc
        p_scr[0:8, :] = jnp.zeros((8, LANE), F32)
        p_scr[8 + ncb:16 + ncb, :] = jnp.zeros((8, LANE), F32)

    for h in range(N_HEADS):
        g, i = divmod(h, I)
        qt_scr[g, :, i * QB:(i + 1) * QB] = (q_ref[:, h * LANE:(h + 1) * LANE] * Q_SCALE).T.astype(BF16)
    small_t = small_ref[...].T
    gates_t = jax.nn.sigmoid(small_t[SMALL_G:SMALL_G + 3 * N_HEADS, :])
    m_scr[...] = jnp.full(m_scr.shape, NEG, F32)
    l_scr[...] = jnp.zeros(l_scr.shape, F32)
    acc_scr[...] = jnp.zeros(acc_scr.shape, F32)

    n_i = lax.broadcasted_iota(jnp.int32, (ncb, QB), 0)
    rr_i = lax.broadcasted_iota(jnp.int32, (ncb, QB), 1)
    dist_c = (rr_i - CMP_STRIDE * n_i - (CMP_BLOCK - 1) + QB * qb).astype(F32)
    mask_c = dist_c >= 0.0
    j_i = lax.broadcasted_iota(jnp.int32, (nsb, QB), 0)
    tq = lax.broadcasted_iota(jnp.int32, (nsb, QB), 1) + QB * qb
    cur = lax.shift_right_arithmetic(tq, 6)
    forced = (j_i == 0) | (j_i == cur) | (j_i == cur - 1)
    valid = j_i * SLC_BLOCK <= tq
    jf = j_i.astype(F32)
    for g in range(G):
        s4 = _mm(kc_ref[0], qt_scr[g])
        psum = jnp.zeros((ncb, QB), F32)
        pns = []
        for i in range(I):
            h = g * I + i
            s = jnp.where(mask_c, s4[:, i * QB:(i + 1) * QB] - SLOPES[h] * dist_c, NEG)
            e = jnp.exp(s - jnp.max(s, axis=0, keepdims=True))
            pn = jnp.where(mask_c, e, 0.0) / jnp.sum(e, axis=0, keepdims=True)
            psum = psum + pn
            pns.append(pn.astype(BF16))
        oc_scr[g] = _mm(vct_ref[0][g * HD:(g + 1) * HD, :], jnp.concatenate(pns, axis=1))
        p_scr[8:8 + ncb, :] = psum
        imp = p_scr[pl.ds(7, nsb, stride=4), :]
        for d in range(1, 5):
            imp = imp + p_scr[pl.ds(7 + d, nsb, stride=4), :]
        score = jnp.where(forced, BIG, jnp.where(valid, imp, NEG))
        sel = jnp.zeros((nsb, QB), F32)
        for _ in range(min(N_SELECT, nsb)):
            cm = jnp.max(score, axis=0, keepdims=True)
            idx = jnp.min(jnp.where(score == cm, jf, 1e9), axis=0, keepdims=True)
            hit = jf == idx
            sel = jnp.where(hit, 1.0, sel)
            score = jnp.where(hit, REMOVED, score)
        sel_scr[g] = sel

    def update(slot, g, s4, mk, off, vt_g):
        m_old = m_scr[slot, 0:1, :]
        l_old = l_scr[slot, 0:1, :]
        ps, m_new, l_new, alphas = [], [], [], []
        for i in range(I):
            h = g * I + i
            s = s4[:, i * QB:(i + 1) * QB] - sd_scr[h]
            if mk is not None:
                s = jnp.where(mk, s, NEG)
            so = SLOPES[h] * off
            mo = m_old[:, i * QB:(i + 1) * QB]
            mn = jnp.maximum(mo, jnp.max(s, axis=0, keepdims=True) - so)
            p = jnp.exp(s - (mn + so))
            al = jnp.exp(mo - mn)
            ps.append(p.astype(BF16))
            m_new.append(mn)
            alphas.append(al)
            l_new.append(al * l_old[:, i * QB:(i + 1) * QB] + jnp.sum(p, axis=0, keepdims=True))
        acc_scr[slot] = jnp.concatenate(alphas, axis=1) * acc_scr[slot] + _mm(vt_g, jnp.concatenate(ps, axis=1))
        m_scr[slot, 0:1, :] = jnp.concatenate(m_new, axis=1)
        l_scr[slot, 0:1, :] = jnp.concatenate(l_new, axis=1)

    def sel_block(kb, diag):
        kblk = ksb[kb]
        vt = vstb[kb]
        off = (qb - kb).astype(F32) * float(QB)
        for g in range(G):
            s4 = _mm(kblk, qt_scr[g])
            m0 = sel_scr[g, pl.ds(2 * kb, 1), :]
            m1 = sel_scr[g, pl.ds(2 * kb + 1, 1), :]
            mk = jnp.concatenate([jnp.broadcast_to(m0, (SLC_BLOCK, QB)), jnp.broadcast_to(m1, (SLC_BLOCK, QB))], axis=0) > 0.5
            if diag:
                mk = mk & (c_i <= r_i)
            update(g, g, s4, mk, off, vt[g * HD:(g + 1) * HD, :])

    def sel_body(kb, carry):
        sel_block(kb, False)
        return carry
    lax.fori_loop(0, qb, sel_body, 0)
    sel_block(qb, True)

    nwb = WINDOW // QB
    for d in range(nwb + 1):
        def win_block(d=d):
            kb = qb - nwb + d
            kblk = kwb[kb]
            vt = vwtb[kb]
            mk = (r_i <= c_i) if d == 0 else ((c_i <= r_i) if d == nwb else None)
            for g in range(G):
                update(G + g, g, _mm(kblk, qt_scr[g]), mk, float((nwb - d) * QB), vt[g * HD:(g + 1) * HD, :])
        if d == nwb:
            win_block()
        else:
            pl.when(qb - nwb + d >= 0)(win_block)

    for pair in range(N_HEADS // 2):
        halves = []
        for h in (2 * pair, 2 * pair + 1):
            g, i = divmod(h, I)
            cols = slice(i * QB, (i + 1) * QB)
            o_s = acc_scr[g, :, cols] / l_scr[g, 0:1, cols]
            o_w = acc_scr[G + g, :, cols] / l_scr[G + g, 0:1, cols]
            o_c = oc_scr[g, :, cols]
            halves.append(gates_t[3 * h:3 * h + 1, :] * o_c + gates_t[3 * h + 1:3 * h + 2, :] * o_s + gates_t[3 * h + 2:3 * h + 3, :] * o_w)
        o_ref[:, pair * LANE:(pair + 1) * LANE] = jnp.concatenate(halves, axis=0).T


def nsa_prompt_attn(B, S, proj, kc_s, vc_st):
    nq = S // Q_BLOCK
    ncb = S // CMP_STRIDE
    nsb = S // SLC_BLOCK
    tok = lambda cb: (lambda b, q: (b * nq + q, cb))
    seq = lambda cb: (lambda b, q: (b, cb))
    one = pl.Buffered(1)
    kvc = COL_KV // LANE
    w4 = HEADS_PER_GROUP * Q_BLOCK
    return pl.pallas_call(
        functools.partial(_nsa_prompt_kernel, S=S),
        grid=(B, nq),
        in_specs=[pl.BlockSpec((Q_BLOCK, N_HEADS * LANE), tok(COL_Q // (N_HEADS * LANE))),
                  pl.BlockSpec((Q_BLOCK, LANE), tok(COL_SMALL // LANE)),
                  pl.BlockSpec((1, ncb, LANE), lambda b, q: (b, 0, 0)),
                  pl.BlockSpec((1, LANE, ncb), lambda b, q: (b, 0, 0)),
                  pl.BlockSpec((S, LANE), seq(kvc + 2), pipeline_mode=one),
                  pl.BlockSpec((S, LANE), seq(kvc + 3), pipeline_mode=one),
                  pl.BlockSpec((S, LANE), seq(kvc + 4), pipeline_mode=one),
                  pl.BlockSpec((S, LANE), seq(kvc + 5), pipeline_mode=one)],
        out_specs=pl.BlockSpec((Q_BLOCK, D_B), tok(0)),
        out_shape=jax.ShapeDtypeStruct((B * S, D_B), F32),
        scratch_shapes=[pltpu.VMEM((nq, Q_BLOCK, LANE), BF16), pltpu.VMEM((nq, LANE, Q_BLOCK), BF16),
                        pltpu.VMEM((nq, Q_BLOCK, LANE), BF16), pltpu.VMEM((nq, LANE, Q_BLOCK), BF16),
                        pltpu.VMEM((KV_GROUPS, LANE, w4), BF16),
                        pltpu.VMEM((N_HEADS, Q_BLOCK, Q_BLOCK), F32),
                        pltpu.VMEM((ncb + 16, Q_BLOCK), F32),
                        pltpu.VMEM((KV_GROUPS, nsb, Q_BLOCK), F32),
                        pltpu.VMEM((2 * KV_GROUPS, 8, w4), F32),
                        pltpu.VMEM((2 * KV_GROUPS, 8, w4), F32),
                        pltpu.VMEM((2 * KV_GROUPS, HEAD_DIM, w4), F32),
                        pltpu.VMEM((KV_GROUPS, HEAD_DIM, w4), F32)],
        compiler_params=pltpu.CompilerParams(dimension_semantics=("arbitrary", "arbitrary"), vmem_limit_bytes=VMEM_LIMIT),
        name="nsa_prompt",
    )(proj, proj, kc_s, vc_st, proj, proj, proj, proj)


N_PAGES = PAST_LEN // PAGE_SIZE
S_NCB = PAST_LEN // CMP_STRIDE
S_NSB = (PAST_LEN + SLC_BLOCK) // SLC_BLOCK
S_NSB_PAD = -(-S_NSB // 8) * 8
PAGES_PER_STEP = 16
CMP_STEPS = N_PAGES // PAGES_PER_STEP
NCOL = DEC_SEQ * N_HEADS
N_DYN = N_SELECT - 3
DYN_ROWS = N_DYN * SLC_BLOCK
ROW_DYN0 = 2 * SLC_BLOCK
ROW_NEW = ROW_DYN0 + DEC_SEQ * KV_GROUPS * DYN_ROWS
NEW_PAD = LANE
ROWS_ALL = ROW_NEW + NEW_PAD
WIN_BUF = min(WINDOW, PAST_LEN)
WIN_ROWS = WIN_BUF + NEW_PAD
assert ROWS_ALL % LANE == 0 and WIN_ROWS % LANE == 0
assert (PAST_LEN + DEC_SEQ - 1) // SLC_BLOCK == S_NSB - 1 and PAST_LEN % SLC_BLOCK == 0 and DEC_SEQ <= 8
assert CMP_STRIDE * S_NCB + CMP_BLOCK - 1 > PAST_LEN + DEC_SEQ - 1


def _col_consts():
    c = np.arange(LANE)
    t = c // N_HEADS
    h = c % N_HEADS
    ok = c < NCOL
    info = np.zeros((8, LANE), np.float32)
    info[0] = np.where(ok, 2.0 ** (-8.0 * (h + 1) / N_HEADS), 0.0)
    info[1] = np.where(ok, PAST_LEN + t, 0)
    info[2] = t
    info[3] = ok
    return info, t, h // HEADS_PER_GROUP, ok


def _sel_mask():
    _, t, g, ok = _col_consts()
    m = np.zeros((ROWS_ALL, LANE), np.float32)
    m[0:ROW_DYN0] = ok[None, :]
    for tt in range(DEC_SEQ):
        for gg in range(KV_GROUPS):
            r0 = ROW_DYN0 + (tt * KV_GROUPS + gg) * DYN_ROWS
            m[r0:r0 + DYN_ROWS] = (ok & (t == tt) & (g == gg))[None, :]
    r = np.arange(NEW_PAD)
    m[ROW_NEW:ROW_NEW + NEW_PAD] = (ok[None, :] & (r[:, None] <= t[None, :]) & (r[:, None] < DEC_SEQ))
    return m


def _compress_paged_kernel(pt_ref, *refs):
    P = PAGES_PER_STEP
    k_pages, k_next = refs[0:P], refs[P]
    v_pages, v_next = refs[P + 1:2 * P + 1], refs[2 * P + 1]
    wk_ref, wv_ref, pk_ref, pv_ref, kc_ref, vct_ref = refs[2 * P + 2:]
    n = P * PAGE_SIZE // CMP_STRIDE
    not_last = (pl.program_id(1) < pl.num_programs(1) - 1).astype(F32)
    last_row = lax.broadcasted_iota(jnp.int32, (n, LANE), 0) == n - 1

    def pooled(pages, nxt, w_ref):
        x = jnp.concatenate([p[0, 0].T for p in pages], axis=0)
        tail_next = jnp.sum(nxt[0, 0].T[0:CMP_STRIDE, :] * w_ref[CMP_STRIDE:CMP_BLOCK, :], axis=0, keepdims=True) * not_last
        return _pool(x, w_ref, n) + jnp.where(last_row, tail_next, 0.0)

    kc_ref[0] = _mm(pooled(k_pages, k_next, wk_ref).astype(BF16), pk_ref[...]).astype(BF16)
    vct_ref[0] = _mm(pooled(v_pages, v_next, wv_ref).astype(BF16), pv_ref[...]).T.astype(BF16)


def compress_paged(l, pool_k, pool_v, page_table, cw):
    B = page_table.shape[0]
    P = PAGES_PER_STEP
    wk, wv, pk, pv = cw
    n = P * PAGE_SIZE // CMP_STRIDE
    page = lambda i: pl.BlockSpec((1, 1, PAGE_SIZE, LANE), lambda b, j, pt: (l, pt[b, j * P + i], 0, 0))
    nxt = pl.BlockSpec((1, 1, PAGE_SIZE, LANE), lambda b, j, pt: (l, pt[b, jnp.minimum(j * P + P, N_PAGES - 1)], 0, 0))
    const = lambda b, j, pt: (0, 0)
    wspecs = [pl.BlockSpec((CMP_BLOCK, LANE), const), pl.BlockSpec((CMP_BLOCK, LANE), const),
              pl.BlockSpec((LANE, LANE), const), pl.BlockSpec((LANE, LANE), const)]
    gs = pltpu.PrefetchScalarGridSpec(
        num_scalar_prefetch=1, grid=(B, CMP_STEPS),
        in_specs=[page(i) for i in range(P)] + [nxt] + [page(i) for i in range(P)] + [nxt] + wspecs,
        out_specs=[pl.BlockSpec((1, n, LANE), lambda b, j, pt: (b, j, 0)), pl.BlockSpec((1, LANE, n), lambda b, j, pt: (b, 0, j))])
    return pl.pallas_call(
        _compress_paged_kernel, grid_spec=gs,
        out_shape=[jax.ShapeDtypeStruct((B, S_NCB, LANE), BF16), jax.ShapeDtypeStruct((B, LANE, S_NCB), BF16)],
        compiler_params=pltpu.CompilerParams(dimension_semantics=("arbitrary", "arbitrary"), vmem_limit_bytes=VMEM_LIMIT),
        name="compress_paged",
    )(page_table, *([pool_k] * (P + 1)), *([pool_v] * (P + 1)), wk, wv, pk, pv)


def _query_cols(q32):
    qp = jnp.concatenate([q32, jnp.zeros((LANE - NCOL, LANE), F32)], axis=0)
    return (qp * Q_SCALE).T.astype(BF16)


def _own_group_lanes(o):
    lane_g = lax.broadcasted_iota(jnp.int32, (NCOL, LANE), 1) // HEAD_DIM
    row_g = (lax.broadcasted_iota(jnp.int32, (NCOL, LANE), 0) % N_HEADS) // HEADS_PER_GROUP
    return jnp.where(lane_g == row_g, o, 0.0)


def _sample_cmp_kernel(q_ref, kc_ref, vct_ref, info_ref, grp_ref, oc_ref, idx_ref, p_scr):
    w = _query_cols(q_ref[0])
    slope = info_ref[0:1, :]
    qpos = info_ref[1:2, :]
    n_f = lax.broadcasted_iota(jnp.int32, (S_NCB, LANE), 0).astype(F32)
    dist = qpos - (CMP_STRIDE * n_f + (CMP_BLOCK - 1))
    mask = dist >= 0.0
    s = jnp.where(mask, _mm(kc_ref[0], w) - slope * dist, NEG)
    e = jnp.exp(s - jnp.max(s, axis=0, keepdims=True))
    pn = jnp.where(mask, e, 0.0) / jnp.sum(e, axis=0, keepdims=True)
    oc_ref[0] = _own_group_lanes(_mm(vct_ref[0], pn.astype(BF16)).T[0:NCOL, :])
    pg = jnp.dot(pn, grp_ref[...], preferred_element_type=F32, precision=HI)
    p_scr[0:8, :] = jnp.zeros((8, LANE), F32)
    p_scr[8:8 + S_NCB, :] = pg
    p_scr[8 + S_NCB:, :] = jnp.zeros((p_scr.shape[0] - 8 - S_NCB, LANE), F32)
    imp = p_scr[pl.ds(7, S_NSB_PAD, stride=4), :]
    for d in range(1, 5):
        imp = imp + p_scr[pl.ds(7 + d, S_NSB_PAD, stride=4), :]
    j_i = lax.broadcasted_iota(jnp.int32, (S_NSB_PAD, LANE), 0)
    tq = PAST_LEN + lax.broadcasted_iota(jnp.int32, (S_NSB_PAD, LANE), 1) // KV_GROUPS
    cur = lax.shift_right_arithmetic(tq, 6)
    forced = (j_i == 0) | (j_i == cur) | (j_i == cur - 1)
    valid = j_i * SLC_BLOCK <= tq
    score = jnp.where(j_i >= S_NSB, REMOVED, jnp.where(forced, BIG, jnp.where(valid, imp, NEG)))
    jf = j_i.astype(F32)
    picks = []
    for _ in range(N_SELECT):
        cm = jnp.max(score, axis=0, keepdims=True)
        idx = jnp.min(jnp.where(score == cm, jf, 1e9), axis=0, keepdims=True)
        score = jnp.where(jf == idx, REMOVED, score)
        picks.append(idx)
    idx_ref[0] = jnp.concatenate(picks, axis=0).astype(jnp.int32)


def sample_cmp(q32, kc_s, vc_st):
    B = q32.shape[0]
    info, t, g, ok = _col_consts()
    grp = np.zeros((LANE, LANE), np.float32)
    for c in range(NCOL):
        grp[c, t[c] * KV_GROUPS + g[c]] = 1.0
    per_b = lambda b: (b, 0, 0)
    const = lambda b: (0, 0)
    return pl.pallas_call(
        _sample_cmp_kernel, grid=(B,),
        in_specs=[pl.BlockSpec((1, NCOL, LANE), per_b), pl.BlockSpec((1, S_NCB, LANE), per_b), pl.BlockSpec((1, LANE, S_NCB), per_b),
                  pl.BlockSpec((8, LANE), const), pl.BlockSpec((LANE, LANE), const)],
        out_specs=[pl.BlockSpec((1, NCOL, LANE), per_b), pl.BlockSpec((1, N_SELECT, LANE), per_b)],
        out_shape=[jax.ShapeDtypeStruct((B, NCOL, LANE), F32), jax.ShapeDtypeStruct((B, N_SELECT, LANE), jnp.int32)],
        scratch_shapes=[pltpu.VMEM((8 + 4 * S_NSB_PAD + 8, LANE), F32)],
        compiler_params=pltpu.CompilerParams(dimension_semantics=("arbitrary",), vmem_limit_bytes=VMEM_LIMIT),
        name="sample_cmp",
    )(q32, kc_s, vc_st, jnp.asarray(info), jnp.asarray(grp))


def _sample_gather_kernel(pt_ref, idx_ref, *refs):
    k_pg, v_pg = refs[0:N_DYN], refs[N_DYN:2 * N_DYN]
    kd_ref, vd_ref, tk, tv = refs[2 * N_DYN:]
    b = pl.program_id(0)
    tg = pl.program_id(1)
    ntg = DEC_SEQ * KV_GROUPS
    for k in range(N_DYN):
        blk = jnp.minimum(idx_ref[b, (k + 3) * ntg + tg], S_NSB - 2)
        r0 = pl.multiple_of((blk % (PAGE_SIZE // SLC_BLOCK)) * SLC_BLOCK, SLC_BLOCK)
        tk[...] = k_pg[k][0, 0].T
        tv[...] = v_pg[k][0, 0].T
        rows = slice(k * SLC_BLOCK, (k + 1) * SLC_BLOCK)
        kd_ref[0, 0, rows, :] = tk[pl.ds(r0, SLC_BLOCK), :].astype(BF16)
        vd_ref[0, 0, rows, :] = tv[pl.ds(r0, SLC_BLOCK), :].astype(BF16)


def sample_gather(l, pool_k, pool_v, page_table, idx):
    B = page_table.shape[0]
    ntg = DEC_SEQ * KV_GROUPS
    per_page = PAGE_SIZE // SLC_BLOCK

    def dyn(k):
        def im(b, tg, pt, ix):
            blk = jnp.minimum(ix[b, (k + 3) * ntg + tg], S_NSB - 2)
            return (l, pt[b, blk // per_page], 0, 0)
        return pl.BlockSpec((1, 1, LANE, PAGE_SIZE), im)
    specs = [dyn(k) for k in range(N_DYN)]
    out = pl.BlockSpec((1, 1, DYN_ROWS, LANE), lambda b, tg, pt, ix: (b, tg, 0, 0))
    gs = pltpu.PrefetchScalarGridSpec(num_scalar_prefetch=2, grid=(B, ntg), in_specs=specs + specs, out_specs=[out, out],
                                      scratch_shapes=[pltpu.VMEM((PAGE_SIZE, LANE), F32), pltpu.VMEM((PAGE_SIZE, LANE), F32)])
    shape = jax.ShapeDtypeStruct((B, ntg, DYN_ROWS, LANE), BF16)
    return pl.pallas_call(
        _sample_gather_kernel, grid_spec=gs, out_shape=[shape, shape],
        compiler_params=pltpu.CompilerParams(dimension_semantics=("arbitrary", "arbitrary"), vmem_limit_bytes=VMEM_LIMIT),
        name="sample_gather",
    )(page_table, idx, *([pool_k] * N_DYN), *([pool_v] * N_DYN))


def _sample_sel_kernel(pt_ref, idx_ref, kd_ref, vd_ref, k0_ref, k1_ref, v0_ref, v1_ref, knew_ref, vnew_ref, kwb_ref, vwb_ref,
                       kwn_ref, vwn_ref, q_ref, gate_ref, oc_ref, info_ref, mask_ref, o_ref, kpos):
    b = pl.program_id(0)
    w = _query_cols(q_ref[0])
    slope = info_ref[0:1, :]
    qpos = info_ref[1:2, :]
    tcol = info_ref[2:3, :]
    okc = info_ref[3:4, :]
    r_f = lax.broadcasted_iota(jnp.int32, (NEW_PAD, LANE), 0).astype(F32)
    cur = (PAST_LEN + DEC_SEQ - 1) // SLC_BLOCK
    ntg = DEC_SEQ * KV_GROUPS
    zpad = jnp.zeros((NEW_PAD - 8, LANE), F32)

    kpos[0:SLC_BLOCK, :] = r_f[0:SLC_BLOCK, :]
    kpos[SLC_BLOCK:ROW_DYN0, :] = float((cur - 1) * SLC_BLOCK) + r_f[0:SLC_BLOCK, :]
    for tg in range(ntg):
        for k in range(N_DYN):
            r0 = ROW_DYN0 + (tg * N_DYN + k) * SLC_BLOCK
            kpos[r0:r0 + SLC_BLOCK, :] = (idx_ref[b, (k + 3) * ntg + tg] * SLC_BLOCK).astype(F32) + r_f[0:SLC_BLOCK, :]
    kpos[ROW_NEW:ROWS_ALL, :] = float(cur * SLC_BLOCK) + r_f

    def gathered(first, prev, dyn_ref, new_ref):
        parts = [first[0, 0].T[0:SLC_BLOCK, :].astype(BF16), prev[0, 0].T[PAGE_SIZE - SLC_BLOCK:, :].astype(BF16)]
        parts += [dyn_ref[0, tg] for tg in range(ntg)]
        return jnp.concatenate(parts + [jnp.concatenate([new_ref[0], zpad], axis=0).astype(BF16)], axis=0)

    def attend(k_rows, v_rows, mask, dist):
        s = jnp.where(mask, _mm(k_rows, w) - slope * dist, NEG)
        p = jnp.exp(s - jnp.max(s, axis=0, keepdims=True))
        l = jnp.sum(p, axis=0, keepdims=True)
        o_t = lax.dot_general(v_rows, p.astype(BF16), (((0,), (0,)), ((), ())), preferred_element_type=F32)
        return (o_t / l).T[0:NCOL, :]

    o_s = attend(gathered(k0_ref, k1_ref, kd_ref, knew_ref), gathered(v0_ref, v1_ref, vd_ref, vnew_ref),
                 mask_ref[...] > 0.5, qpos - kpos[...])
    i_w = lax.broadcasted_iota(jnp.int32, (WIN_ROWS, LANE), 0).astype(F32)
    mask_w = (i_w >= tcol) & (i_w <= tcol + float(WIN_BUF)) & (okc > 0.5)
    kw_rows = jnp.concatenate([kwb_ref[0, 0].T, kwn_ref[0], zpad], axis=0).astype(BF16)
    vw_rows = jnp.concatenate([vwb_ref[0, 0].T, vwn_ref[0], zpad], axis=0).astype(BF16)
    o_w = attend(kw_rows, vw_rows, mask_w, float(WIN_BUF) + tcol - i_w)
    gates = jax.nn.sigmoid(gate_ref[0])
    o_ref[0] = _own_group_lanes(gates[:, 0:1] * oc_ref[0] + gates[:, 1:2] * o_s + gates[:, 2:3] * o_w)


def sample_sel_dense(l, pool_k, pool_v, page_table, idx, knew, vnew, kwin, vwin, kwnew, vwnew, q32, gate32, oc):
    B = page_table.shape[0]
    info, _, _, _ = _col_consts()
    ntg = DEC_SEQ * KV_GROUPS
    per_page = PAGE_SIZE // SLC_BLOCK
    kd, vd = sample_gather(l, pool_k, pool_v, page_table, idx)
    per_b = lambda b, pt, ix: (b, 0, 0)
    const = lambda b, pt, ix: (0, 0)
    dense = pl.BlockSpec((1, ntg, DYN_ROWS, LANE), lambda b, pt, ix: (b, 0, 0, 0))
    first = pl.BlockSpec((1, 1, LANE, PAGE_SIZE), lambda b, pt, ix: (l, pt[b, 0], 0, 0))
    prev = pl.BlockSpec((1, 1, LANE, PAGE_SIZE), lambda b, pt, ix: (l, pt[b, N_PAGES - 1], 0, 0))
    new8 = pl.BlockSpec((1, 8, LANE), per_b)
    win = pl.BlockSpec((1, 1, LANE, WIN_BUF), lambda b, pt, ix: (l, b, 0, 0))
    col32 = pl.BlockSpec((1, NCOL, LANE), per_b)
    gs = pltpu.PrefetchScalarGridSpec(
        num_scalar_prefetch=2, grid=(B,),
        in_specs=[dense, dense, first, prev, first, prev, new8, new8, win, win, new8, new8, col32, col32, col32,
                  pl.BlockSpec((8, LANE), const), pl.BlockSpec((ROWS_ALL, LANE), const)],
        out_specs=col32,
        scratch_shapes=[pltpu.VMEM((ROWS_ALL, LANE), F32)])
    return pl.pallas_call(
        _sample_sel_kernel, grid_spec=gs,
        out_shape=jax.ShapeDtypeStruct((B, NCOL, LANE), F32),
        compiler_params=pltpu.CompilerParams(dimension_semantics=("arbitrary",), vmem_limit_bytes=VMEM_LIMIT),
        name="sample_sel",
    )(page_table, idx, kd, vd, pool_k, pool_k, pool_v, pool_v, knew, vnew, kwin, vwin, kwnew, vwnew,
      q32, gate32, oc, jnp.asarray(info), jnp.asarray(_sel_mask()))


def nsa_sample_attn(l, proj, pools, win, page_table, cw):
    B = page_table.shape[0]
    T = DEC_SEQ
    pool_kc, pool_vc, pool_ks, pool_vs = pools
    n_pool = pool_kc.shape[1]
    pages = lambda p: p.transpose(0, 1, 3, 4, 2).reshape(DEPTH, n_pool, LANE, PAGE_SIZE)
    kc_s, vc_st = compress_paged(l, pages(pool_kc), pages(pool_vc), page_table, cw)
    q32 = proj[:, COL_Q:COL_Q + N_HEADS * LANE].reshape(B, NCOL, LANE)
    oc, idx = sample_cmp(q32, kc_s, vc_st)
    idx = idx[:, :, :T * KV_GROUPS].reshape(B, N_SELECT * T * KV_GROUPS)

    def new8(c):
        return jnp.pad(proj[:, c:c + LANE].reshape(B, T, LANE), ((0, 0), (0, 8 - T), (0, 0)))
    small = proj[:, COL_SMALL:COL_SMALL + LANE].reshape(B, T, LANE)
    gate32 = jnp.pad(small[:, :, SMALL_G:SMALL_G + 3 * N_HEADS].reshape(B, NCOL, 3), ((0, 0), (0, 0), (0, LANE - 3)))
    wins = [w.transpose(0, 1, 3, 4, 2).reshape(DEPTH, B, LANE, WIN_BUF) for w in win]
    o32 = sample_sel_dense(l, pages(pool_ks), pages(pool_vs), page_table, idx, new8(COL_KV + 2 * LANE), new8(COL_KV + 3 * LANE),
                     wins[0], wins[1], new8(COL_KV + 4 * LANE), new8(COL_KV + 5 * LANE), q32, gate32, oc)
    o4 = o32.reshape(B, T, N_HEADS, KV_GROUPS, HEAD_DIM)
    o = jnp.concatenate([o4[:, :, :HEADS_PER_GROUP, 0], o4[:, :, HEADS_PER_GROUP:, 1]], axis=2)
    return o.reshape(B * T, D_B)


def alibi_slopes():
    return jnp.asarray(2.0 ** (-8.0 * np.arange(1, N_HEADS + 1) / N_HEADS), jnp.float32)


def masked_softmax(s, mask):
    p = jax.nn.softmax(jnp.where(mask, s, NEG), axis=-1)
    return jnp.where(mask, p, 0.0)


def pad_rows(a, mult):
    pad = (-a.shape[1]) % mult
    return jnp.pad(a, ((0, 0), (0, pad)) + ((0, 0),) * (a.ndim - 2))


def gather_pages(pool, page_table):
    rows = pool[page_table]
    return rows.reshape((page_table.shape[0], -1) + pool.shape[2:])


def causal_conv(u, buf, w, b):
    T = u.shape[1]
    ext = jnp.concatenate([buf.astype(u.dtype), u], axis=1)
    out = b + ext[:, 0:T] * w[0]
    for j in range(1, CONV_W):
        out = out + ext[:, j:j + T] * w[j]
    return out, ext[:, T:]


def mlstm_chunk(carry, inp):
    C, n, m = carry
    q, k, v, ig, lf = inp
    L = q.shape[1]
    Fh = jnp.cumsum(lf, axis=1).transpose(0, 2, 1)
    igh = ig.transpose(0, 2, 1)
    causal = jnp.tril(jnp.ones((L, L), dtype=bool))
    D = jnp.where(causal, Fh[:, :, :, None] - Fh[:, :, None, :] + igh[:, :, None, :], NEG)
    a = Fh + m[:, :, None]
    m_t = jnp.maximum(a, D.max(-1))
    S = jnp.einsum('bthd,bshd->bhts', q, k) * jnp.exp(D - m_t[..., None])
    inter = jnp.exp(a - m_t)
    num = jnp.einsum('bhts,bshd->bthd', S, v) + inter.transpose(0, 2, 1)[..., None] * jnp.einsum('bthk,bhkv->bthv', q, C)
    den = S.sum(-1) + inter * jnp.einsum('bthk,bhk->bht', q, n)
    denom = jnp.maximum(jnp.abs(den), jnp.exp(-m_t)).transpose(0, 2, 1)[..., None]
    h = num / denom
    F_L = Fh[:, :, -1]
    w_s = F_L[:, :, None] - Fh + igh
    m_new = jnp.maximum(F_L + m, w_s.max(-1))
    decay = jnp.exp(F_L + m - m_new)
    ws = jnp.exp(w_s - m_new[..., None])
    C_new = decay[..., None, None] * C + jnp.einsum('bhs,bshk,bshv->bhkv', ws, k, v)
    n_new = decay[..., None] * n + jnp.einsum('bhs,bshk->bhk', ws, k)
    return (C_new, n_new, m_new), h


def mlstm_branch(qk_pre, v_pre, o_pre, i_pre, f_pre, conv_buf, C, n, m, conv_w, conv_b, head_g):
    f32 = jnp.float32
    B, T = qk_pre.shape[:2]
    qk, conv_new = causal_conv(qk_pre, conv_buf, conv_w, conv_b)
    qk = jax.nn.silu(qk.astype(f32))
    q = qk[..., :D_A].reshape(B, T, M_HEADS, M_HEAD_DIM)
    k = qk[..., D_A:].reshape(B, T, M_HEADS, M_HEAD_DIM) * (M_HEAD_DIM ** -0.5)
    v = v_pre.astype(f32).reshape(B, T, M_HEADS, M_HEAD_DIM)
    ig = i_pre.astype(f32)
    lf = jax.nn.log_sigmoid(f_pre.astype(f32))
    L = M_CHUNK if T % M_CHUNK == 0 else T
    nc = T // L

    def chunks(a):
        return jnp.swapaxes(a.reshape((B, nc, L) + a.shape[2:]), 0, 1)

    (C1, n1, m1), h = lax.scan(mlstm_chunk, (C, n, m), (chunks(q), chunks(k), chunks(v), chunks(ig), chunks(lf)))
    h = jnp.swapaxes(h, 0, 1).reshape(B, T, M_HEADS, M_HEAD_DIM)
    mu = jnp.mean(h, axis=-1, keepdims=True)
    var = jnp.mean(jnp.square(h - mu), axis=-1, keepdims=True)
    hn = ((h - mu) * lax.rsqrt(var + RMS_EPS)).reshape(B, T, D_A) * head_g.astype(f32)
    out = jax.nn.sigmoid(o_pre.astype(f32)) * hn
    return out.astype(qk_pre.dtype), conv_new, C1, n1, m1


def compress(rows, w_pos, w_proj):
    B, Lp = rows.shape[:2]
    sub = rows.reshape(B, Lp // CMP_STRIDE, CMP_STRIDE, KV_GROUPS, HEAD_DIM)
    head = jnp.einsum('bnjgd,jd->bngd', sub, w_pos[:CMP_STRIDE])
    tail = jnp.einsum('bnjgd,jd->bngd', sub, w_pos[CMP_STRIDE:])
    pooled = head[:, :-1] + tail[:, 1:]
    return jnp.einsum('bngd,de->bnge', pooled, w_proj)


def to_blocks(a):
    B, Lp = a.shape[:2]
    return a.reshape(B, Lp // SLC_BLOCK, SLC_BLOCK, KV_GROUPS, HEAD_DIM).transpose(0, 3, 1, 2, 4)


def nsa_attend(q, qpos, g, kc, vc, c_end, ksb, vsb, kw, vw, wpos, slopes):
    f32 = jnp.float32
    B, T = q.shape[:2]
    G, I, dh = KV_GROUPS, HEADS_PER_GROUP, HEAD_DIM
    scale = dh ** -0.5
    qg = q.reshape(B, T, G, I, dh)
    gg = g.reshape(B, T, G, I, 3)
    sl = slopes.reshape(G, I)[None, None, :, :, None]
    tq = qpos.astype(f32)
    dist_c = tq[:, None] - c_end[None, :].astype(f32)
    mask_c = (c_end[None, :] <= qpos[:, None])[None, :, None, None, :]
    s_c = jnp.einsum('btgid,bngd->btgin', qg, kc).astype(f32) * scale - sl * dist_c[None, :, None, None, :]
    p_c = masked_softmax(s_c, mask_c)
    o_c = jnp.einsum('btgin,bngd->btgid', p_c.astype(vc.dtype), vc)
    NC = kc.shape[1]
    NS = ksb.shape[2]
    c_start = jnp.arange(NC, dtype=jnp.int32) * CMP_STRIDE
    s_start = jnp.arange(NS, dtype=jnp.int32) * SLC_BLOCK
    overlap = ((c_start[:, None] < s_start[None, :] + SLC_BLOCK) & (c_start[:, None] + CMP_BLOCK > s_start[None, :])).astype(f32)
    imp = jnp.einsum('btgin,nj->btgj', p_c, overlap)
    blk = jnp.arange(NS, dtype=jnp.int32)[None, :]
    cur = (qpos // SLC_BLOCK)[:, None]
    forced = (blk == 0) | (blk == cur) | (blk == cur - 1)
    valid = s_start[None, :] <= qpos[:, None]
    score = jnp.where(forced[None, :, None, :], BIG, jnp.where(valid[None, :, None, :], imp, NEG))
    n_sel = min(N_SELECT, NS)
    _, idx = lax.top_k(score, n_sel)
    idx_g = idx.transpose(0, 2, 1, 3).reshape(B, G, T * n_sel)
    gather = jax.vmap(jax.vmap(lambda a, i: a[i]))
    kg = gather(ksb, idx_g).reshape(B, G, T, n_sel, SLC_BLOCK, dh)
    vg = gather(vsb, idx_g).reshape(B, G, T, n_sel, SLC_BLOCK, dh)
    pos_s = idx[..., None] * SLC_BLOCK + jnp.arange(SLC_BLOCK, dtype=jnp.int32)
    dist_s = tq[None, :, None, None, None] - pos_s.astype(f32)
    mask_s = (pos_s <= qpos[None, :, None, None, None]).reshape(B, T, G, 1, n_sel * SLC_BLOCK)
    s_s = jnp.einsum('btgid,bgtnpd->btginp', qg, kg).astype(f32) * scale - sl[..., None] * dist_s[:, :, :, None]
    p_s = masked_softmax(s_s.reshape(B, T, G, I, n_sel * SLC_BLOCK), mask_s).reshape(B, T, G, I, n_sel, SLC_BLOCK)
    o_s = jnp.einsum('btginp,bgtnpd->btgid', p_s.astype(vg.dtype), vg)
    delta = qpos[:, None] - wpos[None, :]
    mask_w = ((delta >= 0) & (delta <= WINDOW) & (wpos[None, :] >= 0))[None, :, None, None, :]
    s_w = jnp.einsum('btgid,bwgd->btgiw', qg, kw).astype(f32) * scale - sl * delta.astype(f32)[None, :, None, None, :]
    p_w = masked_softmax(s_w, mask_w)
    o_w = jnp.einsum('btgiw,bwgd->btgid', p_w.astype(vw.dtype), vw)
    o = gg[..., 0:1] * o_c + gg[..., 1:2] * o_s + gg[..., 2:3] * o_w
    return o.reshape(B, T, G * I * dh).astype(q.dtype)


def nsa_prompt(q, g, kc, vc, ks, vs, kw, vw, cmp, slopes):
    B, S = q.shape[:2]
    pos_k, pos_v, proj_k, proj_v = cmp
    kc_s = compress(pad_rows(kc, SLC_BLOCK), pos_k, proj_k)
    vc_s = compress(pad_rows(vc, SLC_BLOCK), pos_v, proj_v)
    c_end = jnp.arange(kc_s.shape[1], dtype=jnp.int32) * CMP_STRIDE + (CMP_BLOCK - 1)
    ksb = to_blocks(pad_rows(ks, SLC_BLOCK))
    vsb = to_blocks(pad_rows(vs, SLC_BLOCK))
    kw_pad = jnp.pad(kw, ((0, 0), (WINDOW, 0), (0, 0), (0, 0)))
    vw_pad = jnp.pad(vw, ((0, 0), (WINDOW, 0), (0, 0), (0, 0)))

    def one_block(start):
        qb = lax.dynamic_slice_in_dim(q, start, Q_BLOCK, axis=1)
        gb = lax.dynamic_slice_in_dim(g, start, Q_BLOCK, axis=1)
        kwb = lax.dynamic_slice_in_dim(kw_pad, start, WINDOW + Q_BLOCK, axis=1)
        vwb = lax.dynamic_slice_in_dim(vw_pad, start, WINDOW + Q_BLOCK, axis=1)
        qpos = start + jnp.arange(Q_BLOCK, dtype=jnp.int32)
        wpos = start - WINDOW + jnp.arange(WINDOW + Q_BLOCK, dtype=jnp.int32)
        return nsa_attend(qb, qpos, gb, kc_s, vc_s, c_end, ksb, vsb, kwb, vwb, wpos, slopes)

    starts = jnp.arange(S // Q_BLOCK, dtype=jnp.int32) * Q_BLOCK
    o = lax.map(one_block, starts)
    o = jnp.moveaxis(o, 0, 1).reshape(B, S, D_B)
    return o


def nsa_sample(q, g, kc, vc, ks, vs, kw, vw, cmp, slopes, pools, win, page_table):
    T = q.shape[1]
    pos_k, pos_v, proj_k, proj_v = cmp
    pool_kc, pool_vc, pool_ks, pool_vs = pools
    buf_kw, buf_vw = win
    past_len = page_table.shape[1] * PAGE_SIZE

    def full(pool, new):
        past = gather_pages(pool, page_table).astype(new.dtype)
        return pad_rows(jnp.concatenate([past, new], axis=1), SLC_BLOCK)

    kc_s = compress(full(pool_kc, kc), pos_k, proj_k)
    vc_s = compress(full(pool_vc, vc), pos_v, proj_v)
    c_end = jnp.arange(kc_s.shape[1], dtype=jnp.int32) * CMP_STRIDE + (CMP_BLOCK - 1)
    ksb = to_blocks(full(pool_ks, ks))
    vsb = to_blocks(full(pool_vs, vs))
    kw_all = jnp.concatenate([buf_kw.astype(kw.dtype), kw], axis=1)
    vw_all = jnp.concatenate([buf_vw.astype(vw.dtype), vw], axis=1)
    wbuf = buf_kw.shape[1]
    wpos = past_len - wbuf + jnp.arange(wbuf + T, dtype=jnp.int32)
    qpos = past_len + jnp.arange(T, dtype=jnp.int32)
    o = nsa_attend(q, qpos, g, kc_s, vc_s, c_end, ksb, vsb, kw_all, vw_all, wpos, slopes)
    return o


def _layer(x, B, T, lw, state, cmp, sample_ctx, g_final, final_norm):
    proj = rms_proj(x, lw["g_mix"], lw["w_in"], lw["b_in"])

    def piece(c0, width):
        return proj[:, c0:c0 + width].reshape(B, T, width)

    m_qk, m_v, m_o = piece(COL_QK, 2 * D_A), piece(COL_V, D_A), piece(COL_O, D_A)
    small = piece(COL_SMALL, LANE)
    kv = [piece(COL_KV + i * KV_W, KV_W).reshape(B, T, KV_GROUPS, HEAD_DIM) for i in range(6)]
    kc, vc, ks, vs, kw, vw = kv

    conv_buf, C, n, m = state
    conv_new = jnp.concatenate([conv_buf, m_qk], axis=1)[:, T:]
    margs = (conv_buf, C, n, m, lw["conv_w"], lw["conv_b"], lw["m_norm"])
    if sample_ctx is None:
        h_a, C1, n1, m1 = mlstm(B, T, proj, COL_QK // (2 * D_A), proj, COL_V // D_A, proj, COL_O // D_A, proj, COL_SMALL // LANE, *margs)
    else:
        def pad_tok(a, fill):
            return jnp.concatenate([a, jnp.broadcast_to(fill, (B, M_CHUNK - T, a.shape[-1]))], axis=1).reshape(B * M_CHUNK, a.shape[-1])
        fill_small = jnp.zeros((LANE,), F32).at[SMALL_I:SMALL_I + M_HEADS].set(NEG).at[SMALL_F:SMALL_F + M_HEADS].set(BIG)
        zero = jnp.zeros((1,), F32)
        h_pad, C1, n1, m1 = mlstm(B, M_CHUNK, pad_tok(m_qk, zero), 0, pad_tok(m_v, zero), 0, pad_tok(m_o, zero), 0, pad_tok(small, fill_small), 0, *margs)
        h_a = h_pad.reshape(B, M_CHUNK, D_A)[:, :T].reshape(B * T, D_A)
    cw = _cmp_weights(*cmp)
    if sample_ctx is None:
        kvc = COL_KV // LANE
        kc_s, vc_st = compress_prompt(B, T, proj, kvc, proj, kvc + 1, cw)
        o_b = nsa_prompt_attn(B, T, proj, kc_s, vc_st)
        keep = min(WINDOW, T)
        kw_state, vw_state = kw[:, T - keep:], vw[:, T - keep:]
    else:
        l, pools, win, page_table = sample_ctx
        o_b = nsa_sample_attn(l, proj, pools, win, page_table, cw)
        keep = min(WINDOW, WIN_BUF + T)
        kw_state = jnp.concatenate([win[0][l], kw], axis=1)[:, WIN_BUF + T - keep:]
        vw_state = jnp.concatenate([win[1][l], vw], axis=1)[:, WIN_BUF + T - keep:]

    x = merge(x, h_a.reshape(B * T, D_A), o_b.reshape(B * T, D_B), proj, lw["w_a"], lw["w_b"], lw["w_o"])
    x = ffn(x, lw["g_ffn"], lw["w_ffn_in"], lw["w_ffn_out"], g_final, final_norm)
    return x, (kc, vc, ks, vs, kw_state, vw_state, conv_new, C1, n1, m1)


def kernel(x_prompt, x_sample, cache_k_cmp, cache_v_cmp, cache_k_slc, cache_v_slc, state_k_win, state_v_win, state_conv, state_C, state_n, state_m, page_table, norm_mix, w_in, b_in, conv_w, conv_b, cmp_pos_k, cmp_pos_v, cmp_proj_k, cmp_proj_v, m_norm, w_a, w_b, w_o, norm_ffn, w_ffn_in, w_ffn_out, norm_final):
    yp = x_prompt.reshape(BATCH * SEQ, D_MODEL)
    ys = x_sample.reshape(DEC_BATCH * DEC_SEQ, D_MODEL)
    p_states, s_states = [], []
    for l in range(DEPTH):
        w_in_l, b_in_l = _prep_w_in(w_in[l], b_in[l])
        lw = dict(g_mix=norm_mix[l], w_in=w_in_l, b_in=b_in_l, conv_w=conv_w[l], conv_b=conv_b[l], m_norm=m_norm[l],
                  w_a=w_a[l].astype(BF16), w_b=w_b[l].astype(BF16), w_o=w_o[l].astype(BF16), g_ffn=norm_ffn[l],
                  w_ffn_in=w_ffn_in[l].astype(BF16), w_ffn_out=w_ffn_out[l].astype(BF16))
        cmp = (cmp_pos_k[l], cmp_pos_v[l], cmp_proj_k[l], cmp_proj_v[l])
        last = l == DEPTH - 1
        conv0 = jnp.zeros((BATCH, CONV_W - 1, 2 * D_A), F32)
        C0 = jnp.zeros((BATCH, M_HEADS, M_HEAD_DIM, M_HEAD_DIM), F32)
        n0 = jnp.zeros((BATCH, M_HEADS, M_HEAD_DIM), F32)
        m0 = jnp.full((BATCH, M_HEADS), NEG, F32)
        yp, st_p = _layer(yp, BATCH, SEQ, lw, (conv0, C0, n0, m0), cmp, None, norm_final, last)
        p_states.append(st_p)
        ctx = (l, (cache_k_cmp, cache_v_cmp, cache_k_slc, cache_v_slc), (state_k_win, state_v_win), page_table)
        ys, st_s = _layer(ys, DEC_BATCH, DEC_SEQ, lw, (state_conv[l], state_C[l], state_n[l], state_m[l]), cmp, ctx, norm_final, last)
        s_states.append(st_s)
    y_prompt = yp.reshape(BATCH, SEQ, D_MODEL)
    y_sample = ys.reshape(DEC_BATCH, DEC_SEQ, D_MODEL)
    P = [jnp.stack(a) for a in zip(*p_states)]
    S = [jnp.stack(a) for a in zip(*s_states)]
    return (y_prompt, y_sample, P[0], S[0], P[1], S[1], P[2], S[2], P[3], S[3], P[4], S[4],
            P[5], S[5], P[6], S[6], P[7], S[7], P[8], S[8], P[9], S[9])
```

```python
import functools
import jax, jax.numpy as jnp
from jax import lax
import numpy as np
from jax.experimental import pallas as pl
from jax.experimental.pallas import tpu as pltpu

D_MODEL = 1024
BATCH = 2
SEQ = 8192
DEPTH = 2
DEC_BATCH = 32
DEC_SEQ = 4
PAST_LEN = 16384
PAGE_SIZE = 128
M_HEADS = 4
M_HEAD_DIM = D_MODEL // 8
D_A = M_HEADS * M_HEAD_DIM
CONV_W = 4
M_CHUNK = 128
N_HEADS = 8
HEAD_DIM = D_MODEL // 16
D_B = N_HEADS * HEAD_DIM
KV_GROUPS = 2
HEADS_PER_GROUP = N_HEADS // KV_GROUPS
KV_W = KV_GROUPS * HEAD_DIM
CMP_STRIDE = 16
CMP_BLOCK = 2 * CMP_STRIDE
SLC_BLOCK = 64
N_SELECT = 16
WINDOW = 512
Q_BLOCK = 128
D_FF = -(-8 * D_MODEL // (3 * 256)) * 256
RMS_EPS = 1e-6
NEG = -1e30
BIG = 1e30
SIZES = (2 * D_A, D_A, D_A, M_HEADS, M_HEADS, D_B, KV_W, KV_W, KV_W, KV_W, KV_W, KV_W, 3 * N_HEADS, 2 * D_MODEL)
SPLITS = tuple(int(s) for s in np.cumsum(SIZES)[:-1])
D_IN = sum(SIZES)

F32 = jnp.float32
BF16 = jnp.bfloat16
LANE = 128
VMEM_LIMIT = 48 * 1024 * 1024

COL_QK = 0
COL_GMA = 1024
COL_GMB = 2048
COL_V = 3072
COL_O = 3584
COL_Q = 4096
COL_KV = 5120
COL_SMALL = 5888
N_PROJ = 6144
SMALL_I = 0
SMALL_F = M_HEADS
SMALL_G = 2 * M_HEADS


def _prep_w_in(w, b):
    def cols(a):
        (m_qk, m_v, m_o, m_i, m_f, a_q, a_kc, a_vc, a_ks, a_vs, a_kw, a_vw, a_g, g_mrg) = jnp.split(a, SPLITS, axis=-1)
        z64 = jnp.zeros(a.shape[:-1] + (HEAD_DIM,), a.dtype)
        slabs = []
        for h in range(N_HEADS):
            qh = a_q[..., h * HEAD_DIM:(h + 1) * HEAD_DIM]
            slabs += [qh, z64] if h < HEADS_PER_GROUP else [z64, qh]
        small = jnp.concatenate([m_i, m_f, a_g, jnp.zeros(a.shape[:-1] + (LANE - 2 * M_HEADS - 3 * N_HEADS,), a.dtype)], -1)
        pad = jnp.zeros(a.shape[:-1] + (N_PROJ - COL_SMALL - LANE,), a.dtype)
        return jnp.concatenate([m_qk, g_mrg, m_v, m_o] + slabs + [a_kc, a_vc, a_ks, a_vs, a_kw, a_vw, small, pad], -1)
    return cols(w).astype(BF16), cols(b[None, :])


def _rms_proj_kernel(x_ref, g_ref, w_ref, b_ref, o_ref, h_scr):
    @pl.when(pl.program_id(1) == 0)
    def _():
        x = x_ref[...]
        ms = jnp.mean(x * x, axis=-1, keepdims=True)
        h_scr[...] = (x * lax.rsqrt(ms + RMS_EPS) * g_ref[...]).astype(BF16)
    o_ref[...] = jnp.dot(h_scr[...], w_ref[...], preferred_element_type=F32) + b_ref[...]


def rms_proj(x, g, w, b):
    T, D = x.shape
    N = w.shape[1]
    tm = min(T, 1024)
    tn = 2048
    return pl.pallas_call(
        _rms_proj_kernel,
        grid=(T // tm, N // tn),
        in_specs=[pl.BlockSpec((tm, D), lambda i, j: (i, 0)),
                  pl.BlockSpec((1, D), lambda i, j: (0, 0)),
                  pl.BlockSpec((D, tn), lambda i, j: (0, j)),
                  pl.BlockSpec((1, tn), lambda i, j: (0, j))],
        out_specs=pl.BlockSpec((tm, tn), lambda i, j: (i, j)),
        out_shape=jax.ShapeDtypeStruct((T, N), F32),
        scratch_shapes=[pltpu.VMEM((tm, D), BF16)],
        compiler_params=pltpu.CompilerParams(dimension_semantics=("arbitrary", "arbitrary"), vmem_limit_bytes=VMEM_LIMIT),
        name="rms_proj",
    )(x, g.reshape(1, D), w, b)


def _merge_kernel(x_ref, ha_ref, ob_ref, ga_ref, gb_ref, wa_ref, wb_ref, wo_ref, o_ref):
    ya = jnp.dot(ha_ref[...].astype(BF16), wa_ref[...], preferred_element_type=F32)
    yb = jnp.dot(ob_ref[...].astype(BF16), wb_ref[...], preferred_element_type=F32)
    y = jax.nn.sigmoid(ga_ref[...]) * ya + jax.nn.sigmoid(gb_ref[...]) * yb
    o_ref[...] = x_ref[...] + jnp.dot(y.astype(BF16), wo_ref[...], preferred_element_type=F32)


def merge(x, ha, ob, proj, wa, wb, wo):
    T, D = x.shape
    tm = min(T, 512)
    row = lambda i: (i, 0)
    const = lambda i: (0, 0)
    return pl.pallas_call(
        _merge_kernel,
        grid=(T // tm,),
        in_specs=[pl.BlockSpec((tm, D), row),
                  pl.BlockSpec((tm, D_A), row),
                  pl.BlockSpec((tm, D_B), row),
                  pl.BlockSpec((tm, D), lambda i: (i, COL_GMA // D_MODEL)),
                  pl.BlockSpec((tm, D), lambda i: (i, COL_GMB // D_MODEL)),
                  pl.BlockSpec((D_A, D), const),
                  pl.BlockSpec((D_B, D), const),
                  pl.BlockSpec((D, D), const)],
        out_specs=pl.BlockSpec((tm, D), row),
        out_shape=jax.ShapeDtypeStruct((T, D), F32),
        compiler_params=pltpu.CompilerParams(dimension_semantics=("arbitrary",), vmem_limit_bytes=VMEM_LIMIT),
        name="merge",
    )(x, ha, ob, proj, proj, wa, wb, wo)


def _ffn_kernel(x_ref, g_ref, wg_ref, wu_ref, wo_ref, gf_ref, o_ref, h_scr, acc_scr, *, final_norm):
    j = pl.program_id(1)

    @pl.when(j == 0)
    def _():
        x = x_ref[...]
        ms = jnp.mean(x * x, axis=-1, keepdims=True)
        h_scr[...] = (x * lax.rsqrt(ms + RMS_EPS) * g_ref[...]).astype(BF16)
        acc_scr[...] = jnp.zeros_like(acc_scr)

    h = h_scr[...]
    gate = jnp.dot(h, wg_ref[...], preferred_element_type=F32)
    up = jnp.dot(h, wu_ref[...], preferred_element_type=F32)
    act = (gate * jax.nn.sigmoid(gate) * up).astype(BF16)
    acc_scr[...] += jnp.dot(act, wo_ref[...], preferred_element_type=F32)

    @pl.when(j == pl.num_programs(1) - 1)
    def _():
        y = x_ref[...] + acc_scr[...]
        if final_norm:
            ms = jnp.mean(y * y, axis=-1, keepdims=True)
            y = y * lax.rsqrt(ms + RMS_EPS) * gf_ref[...]
        o_ref[...] = y


def ffn(x, g, w_in, w_out, g_final, final_norm):
    T, D = x.shape
    tm = min(T, 512)
    tf = D_FF // 2
    nf = D_FF // tf
    return pl.pallas_call(
        functools.partial(_ffn_kernel, final_norm=final_norm),
        grid=(T // tm, nf),
        in_specs=[pl.BlockSpec((tm, D), lambda i, j: (i, 0)),
                  pl.BlockSpec((1, D), lambda i, j: (0, 0)),
                  pl.BlockSpec((D, tf), lambda i, j: (0, j)),
                  pl.BlockSpec((D, tf), lambda i, j: (0, j + nf)),
                  pl.BlockSpec((tf, D), lambda i, j: (j, 0)),
                  pl.BlockSpec((1, D), lambda i, j: (0, 0))],
        out_specs=pl.BlockSpec((tm, D), lambda i, j: (i, 0)),
        out_shape=jax.ShapeDtypeStruct((T, D), F32),
        scratch_shapes=[pltpu.VMEM((tm, D), BF16), pltpu.VMEM((tm, D), F32)],
        compiler_params=pltpu.CompilerParams(dimension_semantics=("arbitrary", "arbitrary"), vmem_limit_bytes=VMEM_LIMIT),
        name="ffn",
    )(x, g.reshape(1, D), w_in, w_in, w_out, g_final.reshape(1, D))


HI = lax.Precision.HIGHEST


def _mm(a, b):
    return jnp.dot(a, b, preferred_element_type=F32)


def _log_sigmoid(x):
    return jnp.minimum(x, 0.0) - jnp.log1p(jnp.exp(-jnp.abs(x)))


def _mlstm_kernel(qk_ref, v_ref, o_ref, small_ref, cbuf_ref, c0_ref, n0_ref, m0_ref, cw_ref, cb_ref, g_ref,
                  h_ref, c_out_ref, n_out_ref, m_out_ref, ext_scr, c_scr, n_scr, m_scr):
    L = M_CHUNK
    c = pl.program_id(1)

    @pl.when(c == 0)
    def _():
        ext_scr[0:8, :] = cbuf_ref[0]
        c_scr[...] = c0_ref[0]
        n_scr[...] = n0_ref[0]
        m_scr[...] = m0_ref[0]

    ext_scr[8:8 + L, :] = qk_ref[...]
    conv = cb_ref[...] + ext_scr[5:5 + L, :] * cw_ref[0:1, :]
    for j in range(1, CONV_W):
        conv = conv + ext_scr[5 + j:5 + j + L, :] * cw_ref[j:j + 1, :]
    ext_scr[0:8, :] = qk_ref[L - 8:L, :]
    qk = conv * jax.nn.sigmoid(conv)

    small = small_ref[...]
    small_t = small.T
    row = lax.broadcasted_iota(jnp.int32, (L, L), 0)
    col = lax.broadcasted_iota(jnp.int32, (L, L), 1)
    causal = col <= row
    fcol_all = jnp.dot(causal.astype(F32), _log_sigmoid(small), preferred_element_type=F32, precision=HI)
    frow_all = jnp.dot(_log_sigmoid(small_t), (row <= col).astype(F32), preferred_element_type=F32, precision=HI)

    for h in range(M_HEADS):
        sl = slice(h * M_HEAD_DIM, (h + 1) * M_HEAD_DIM)
        q = qk[:, sl]
        k = qk[:, D_A + h * M_HEAD_DIM:D_A + (h + 1) * M_HEAD_DIM] * (M_HEAD_DIM ** -0.5)
        fc = fcol_all[:, SMALL_F + h:SMALL_F + h + 1]
        fr = frow_all[SMALL_F + h:SMALL_F + h + 1, :]
        igr = small_t[SMALL_I + h:SMALL_I + h + 1, :]
        igc = small[:, SMALL_I + h:SMALL_I + h + 1]
        m_prev = m_scr[h:h + 1, 0:1]
        d = jnp.where(causal, fc - fr + igr, NEG)
        a = fc + m_prev
        m_t = jnp.maximum(a, jnp.max(d, axis=-1, keepdims=True))
        qb = q.astype(BF16)
        kb = k.astype(BF16)
        vb = v_ref[:, sl].astype(BF16)
        s = lax.dot_general(qb, kb, (((1,), (1,)), ((), ())), preferred_element_type=F32) * jnp.exp(d - m_t)
        inter = jnp.exp(a - m_t)
        num = _mm(s.astype(BF16), vb) + inter * _mm(qb, c_scr[h].astype(BF16))
        den = jnp.sum(s, axis=-1, keepdims=True) + inter * jnp.sum(q * n_scr[h:h + 1, :], axis=-1, keepdims=True)
        hh = num / jnp.maximum(jnp.abs(den), jnp.exp(-m_t))
        mu = jnp.mean(hh, axis=-1, keepdims=True)
        var = jnp.mean(jnp.square(hh - mu), axis=-1, keepdims=True)
        hn = (hh - mu) * lax.rsqrt(var + RMS_EPS) * g_ref[:, sl]
        h_ref[:, sl] = jax.nn.sigmoid(o_ref[:, sl]) * hn
        f_l = fr[:, L - 1:L]
        m_new = jnp.maximum(f_l + m_prev, jnp.max(f_l - fr + igr, axis=-1, keepdims=True))
        decay = jnp.exp(f_l + m_prev - m_new)
        ks = k * jnp.exp(f_l - fc + igc - m_new)
        c_scr[h] = decay * c_scr[h] + lax.dot_general(ks.astype(BF16), vb, (((0,), (0,)), ((), ())), preferred_element_type=F32)
        n_scr[h:h + 1, :] = decay * n_scr[h:h + 1, :] + jnp.sum(ks, axis=0, keepdims=True)
        m_scr[h:h + 1, :] = jnp.broadcast_to(m_new, (1, LANE))

    @pl.when(c == pl.num_programs(1) - 1)
    def _():
        c_out_ref[0] = c_scr[...]
        n_out_ref[0] = n_scr[...]
        m_out_ref[0] = m_scr[...]


def mlstm(B, T, qk, qk_col, v, v_col, o, o_col, small, small_col, conv_buf, C0, n0, m0, conv_w, conv_b, head_g):
    L = M_CHUNK
    nch = T // L
    cbuf = jnp.concatenate([jnp.zeros((B, 8 - (CONV_W - 1), 2 * D_A), F32), conv_buf], axis=1)
    n0p = jnp.concatenate([n0, jnp.zeros((B, 8 - M_HEADS, M_HEAD_DIM), F32)], axis=1)
    m0p = jnp.broadcast_to(jnp.concatenate([m0, jnp.zeros((B, 8 - M_HEADS), F32)], axis=1)[:, :, None], (B, 8, LANE))
    tok = lambda cb: (lambda b, c: (b * nch + c, cb))
    per_b3 = lambda b, c: (b, 0, 0)
    per_b4 = lambda b, c: (b, 0, 0, 0)
    const = lambda b, c: (0, 0)
    h, C1, n1, m1 = pl.pallas_call(
        _mlstm_kernel,
        grid=(B, nch),
        in_specs=[pl.BlockSpec((L, 2 * D_A), tok(qk_col)),
                  pl.BlockSpec((L, D_A), tok(v_col)),
                  pl.BlockSpec((L, D_A), tok(o_col)),
                  pl.BlockSpec((L, LANE), tok(small_col)),
                  pl.BlockSpec((1, 8, 2 * D_A), per_b3),
                  pl.BlockSpec((1, M_HEADS, M_HEAD_DIM, M_HEAD_DIM), per_b4),
                  pl.BlockSpec((1, 8, M_HEAD_DIM), per_b3),
                  pl.BlockSpec((1, 8, LANE), per_b3),
                  pl.BlockSpec((CONV_W, 2 * D_A), const),
                  pl.BlockSpec((1, 2 * D_A), const),
                  pl.BlockSpec((1, D_A), const)],
        out_specs=[pl.BlockSpec((L, D_A), tok(0)),
                   pl.BlockSpec((1, M_HEADS, M_HEAD_DIM, M_HEAD_DIM), per_b4),
                   pl.BlockSpec((1, 8, M_HEAD_DIM), per_b3),
                   pl.BlockSpec((1, 8, LANE), per_b3)],
        out_shape=[jax.ShapeDtypeStruct((B * T, D_A), F32),
                   jax.ShapeDtypeStruct((B, M_HEADS, M_HEAD_DIM, M_HEAD_DIM), F32),
                   jax.ShapeDtypeStruct((B, 8, M_HEAD_DIM), F32),
                   jax.ShapeDtypeStruct((B, 8, LANE), F32)],
        scratch_shapes=[pltpu.VMEM((8 + L, 2 * D_A), F32),
                        pltpu.VMEM((M_HEADS, M_HEAD_DIM, M_HEAD_DIM), F32),
                        pltpu.VMEM((8, M_HEAD_DIM), F32),
                        pltpu.VMEM((8, LANE), F32)],
        compiler_params=pltpu.CompilerParams(dimension_semantics=("arbitrary", "arbitrary"), vmem_limit_bytes=VMEM_LIMIT),
        name="mlstm",
    )(qk, v, o, small, cbuf, C0, n0p, m0p, conv_w, conv_b.reshape(1, -1), head_g.reshape(1, -1))
    return h, C1, n1[:, :M_HEADS], m1[:, :M_HEADS, 0]


SLOPES = tuple(float(2.0 ** (-8.0 * (h + 1) / N_HEADS)) for h in range(N_HEADS))
Q_SCALE = HEAD_DIM ** -0.5
REMOVED = -3.0e38


def _pool(x, w_ref, n):
    x3 = x.reshape(n, CMP_STRIDE, LANE)
    head = jnp.sum(x3 * w_ref[0:CMP_STRIDE, :][None], axis=1)
    tail = jnp.sum(x3 * w_ref[CMP_STRIDE:CMP_BLOCK, :][None], axis=1)
    nxt = pltpu.roll(tail, n - 1, 0)
    last = lax.broadcasted_iota(jnp.int32, (n, LANE), 0) == n - 1
    return head + jnp.where(last, 0.0, nxt)


def _compress_kernel(k_ref, v_ref, wk_ref, wv_ref, pk_ref, pv_ref, kc_ref, vct_ref, *, n):
    pk = _pool(k_ref[...], wk_ref, n)
    pv = _pool(v_ref[...], wv_ref, n)
    kc_ref[0] = _mm(pk.astype(BF16), pk_ref[...]).astype(BF16)
    vct_ref[0] = _mm(pv.astype(BF16), pv_ref[...]).T.astype(BF16)


def _cmp_weights(pos_k, pos_v, proj_k, proj_v):
    tile = lambda w: jnp.concatenate([w] * KV_GROUPS, axis=1)
    bd = lambda w: jnp.kron(jnp.eye(KV_GROUPS, dtype=F32), w).astype(BF16)
    return tile(pos_k), tile(pos_v), bd(proj_k), bd(proj_v)


def compress_prompt(B, S, kc, kc_col, vc, vc_col, cw):
    n = S // CMP_STRIDE
    wk, wv, pk, pv = cw
    const = lambda b: (0, 0)
    return pl.pallas_call(
        functools.partial(_compress_kernel, n=n),
        grid=(B,),
        in_specs=[pl.BlockSpec((S, LANE), lambda b: (b, kc_col)),
                  pl.BlockSpec((S, LANE), lambda b: (b, vc_col)),
                  pl.BlockSpec((CMP_BLOCK, LANE), const), pl.BlockSpec((CMP_BLOCK, LANE), const),
                  pl.BlockSpec((LANE, LANE), const), pl.BlockSpec((LANE, LANE), const)],
        out_specs=[pl.BlockSpec((1, n, LANE), lambda b: (b, 0, 0)), pl.BlockSpec((1, LANE, n), lambda b: (b, 0, 0))],
        out_shape=[jax.ShapeDtypeStruct((B, n, LANE), BF16), jax.ShapeDtypeStruct((B, LANE, n), BF16)],
        compiler_params=pltpu.CompilerParams(dimension_semantics=("arbitrary",), vmem_limit_bytes=VMEM_LIMIT),
        name="compress_prompt",
    )(kc, vc, wk, wv, pk, pv)


def _nsa_prompt_kernel(q_ref, small_ref, kc_ref, vct_ref, ks_ref, vs_ref, kw_ref, vw_ref, o_ref,
                       ksb, vstb, kwb, vwtb, qt_scr, sd_scr, p_scr, sel_scr, m_scr, l_scr, acc_scr, oc_scr, *, S):
    QB = Q_BLOCK
    nkb = S // QB
    assert nkb <= 64
    ncb = S // CMP_STRIDE
    nsb = S // SLC_BLOCK
    G, I, HD = KV_GROUPS, HEADS_PER_GROUP, HEAD_DIM
    qb = pl.program_id(1)
    c_i = lax.broadcasted_iota(jnp.int32, (QB, QB), 0)
    r_i = lax.broadcasted_iota(jnp.int32, (QB, QB), 1)

    @pl.when(qb == 0)
    def _():
        def cp(i, carry):
            rows = pl.ds(pl.multiple_of(i * QB, QB), QB)
            ksb[i] = ks_ref[rows, :].astype(BF16)
            kwb[i] = kw_ref[rows, :].astype(BF16)
            vstb[i] = vs_ref[rows, :].T.astype(BF16)
            vwtb[i] = vw_ref[rows, :].T.astype(BF16)
            return carry
        lax.fori_loop(0, nkb, cp, 0)
        rc = (r_i - c_i).astype(F32)
        for h in range(N_HEADS):
            sd_scr[h] = SLOPES[h] * rc
        p_scr[0:8, :] = jnp.zeros((8, LANE), F32)
        p_scr[8 + ncb:16 + ncb, :] = jnp.zeros((8, LANE), F32)

    for h in range(N_HEADS):
        g, i = divmod(h, I)
        qt_scr[g, :, i * QB:(i + 1) * QB] = (q_ref[:, h * LANE:(h + 1) * LANE] * Q_SCALE).T.astype(BF16)
    small_t = small_ref[...].T
    gates_t = jax.nn.sigmoid(small_t[SMALL_G:SMALL_G + 3 * N_HEADS, :])
    m_scr[...] = jnp.full(m_scr.shape, NEG, F32)
    l_scr[...] = jnp.zeros(l_scr.shape, F32)
    acc_scr[...] = jnp.zeros(acc_scr.shape, F32)

    n_i = lax.broadcasted_iota(jnp.int32, (ncb, QB), 0)
    rr_i = lax.broadcasted_iota(jnp.int32, (ncb, QB), 1)
    dist_c = (rr_i - CMP_STRIDE * n_i - (CMP_BLOCK - 1) + QB * qb).astype(F32)
    mask_c = dist_c >= 0.0
    j_i = lax.broadcasted_iota(jnp.int32, (nsb, QB), 0)
    tq = lax.broadcasted_iota(jnp.int32, (nsb, QB), 1) + QB * qb
    cur = lax.shift_right_arithmetic(tq, 6)
    forced = (j_i == 0) | (j_i == cur) | (j_i == cur - 1)
    valid = j_i * SLC_BLOCK <= tq
    jf = j_i.astype(F32)
    jb_i = lax.broadcasted_iota(jnp.int32, (nsb, 1), 0)
    need_words = []
    for g in range(G):
        s4 = _mm(kc_ref[0], qt_scr[g])
        psum = jnp.zeros((ncb, QB), F32)
        pns = []
        for i in range(I):
            h = g * I + i
            s = jnp.where(mask_c, s4[:, i * QB:(i + 1) * QB] - SLOPES[h] * dist_c, NEG)
            e = jnp.exp(s - jnp.max(s, axis=0, keepdims=True))
            pn = jnp.where(mask_c, e, 0.0) / jnp.sum(e, axis=0, keepdims=True)
            psum = psum + pn
            pns.append(pn.astype(BF16))
        oc_scr[g] = _mm(vct_ref[0][g * HD:(g + 1) * HD, :], jnp.concatenate(pns, axis=1))
        p_scr[8:8 + ncb, :] = psum
        imp = p_scr[pl.ds(7, nsb, stride=4), :]
        for d in range(1, 5):
            imp = imp + p_scr[pl.ds(7 + d, nsb, stride=4), :]
        score = jnp.where(forced, BIG, jnp.where(valid, imp, NEG))
        sel = jnp.zeros((nsb, QB), F32)
        for _ in range(min(N_SELECT, nsb)):
            cm = jnp.max(score, axis=0, keepdims=True)
            idx = jnp.min(jnp.where(score == cm, jf, 1e9), axis=0, keepdims=True)
            hit = jf == idx
            sel = jnp.where(hit, 1.0, sel)
            score = jnp.where(hit, REMOVED, score)
        sel_scr[g] = sel
        used = jnp.max(sel, axis=1, keepdims=True) > 0.5
        kb_j = lax.shift_right_logical(jb_i, 1)
        bits = jnp.where(used, lax.shift_left(jnp.ones_like(jb_i), kb_j & 31), 0)

        def word(half, parity):
            return jnp.sum(jnp.where((lax.shift_right_logical(kb_j, 5) == half) & ((jb_i & 1) == parity), bits, 0))
        need_words.append((word(0, 0) | word(0, 1), word(1, 0) | word(1, 1)))

    def update(slot, g, kblk, mk, off, vt_g):
        m_all = m_scr[slot, 0:1, :]
        l_all = l_scr[slot, 0:1, :]
        acc_all = acc_scr[slot]
        s4 = _mm(kblk, qt_scr[g])
        m_out, l_out, acc_out = [], [], []
        for pair in range(I // 2):
            cols = slice(pair * 2 * QB, (pair + 1) * 2 * QB)
            s2 = s4[:, cols]
            m_old = m_all[:, cols]
            l_old = l_all[:, cols]
            ps, alphas = [], []
            for j in range(2):
                h = g * I + pair * 2 + j
                s = s2[:, j * QB:(j + 1) * QB] - sd_scr[h]
                if mk is not None:
                    s = jnp.where(mk, s, NEG)
                so = SLOPES[h] * off
                mo = m_old[:, j * QB:(j + 1) * QB]
                mn = jnp.maximum(mo, jnp.max(s, axis=0, keepdims=True) - so)
                p = jnp.exp(s - (mn + so))
                al = jnp.exp(mo - mn)
                ps.append(p.astype(BF16))
                m_out.append(mn)
                alphas.append(al)
                l_out.append(al * l_old[:, j * QB:(j + 1) * QB] + jnp.sum(p, axis=0, keepdims=True))
            acc_out.append(jnp.concatenate(alphas, axis=1) * acc_all[:, cols] + _mm(vt_g, jnp.concatenate(ps, axis=1)))
        acc_scr[slot] = jnp.concatenate(acc_out, axis=1)
        m_scr[slot, 0:1, :] = jnp.concatenate(m_out, axis=1)
        l_scr[slot, 0:1, :] = jnp.concatenate(l_out, axis=1)

    def sel_block(kb, g, diag):
        off = (qb - kb).astype(F32) * float(QB)
        m0 = sel_scr[g, pl.ds(2 * kb, 1), :]
        m1 = sel_scr[g, pl.ds(2 * kb + 1, 1), :]
        mk = jnp.concatenate([jnp.broadcast_to(m0, (SLC_BLOCK, QB)), jnp.broadcast_to(m1, (SLC_BLOCK, QB))], axis=0) > 0.5
        if diag:
            mk = mk & (c_i <= r_i)
        update(g, g, ksb[kb], mk, off, vstb[kb, g * HD:(g + 1) * HD, :])

    for g in range(G):
        lo, hi = need_words[g]

        def sel_body(kb, carry, g=g, lo=lo, hi=hi):
            bit = lax.shift_right_logical(jnp.where(kb < 32, lo, hi), kb & 31) & 1
            pl.when(bit != 0)(functools.partial(sel_block, kb, g, False))
            return carry
        lax.fori_loop(0, qb, sel_body, 0)
    for g in range(G):
        sel_block(qb, g, True)

    nwb = WINDOW // QB
    for d in range(nwb + 1):
        def win_block(d=d):
            kb = qb - nwb + d
            kblk = kwb[kb]
            vt = vwtb[kb]
            mk = (r_i <= c_i) if d == 0 else ((c_i <= r_i) if d == nwb else None)
            for g in range(G):
                update(G + g, g, kblk, mk, float((nwb - d) * QB), vt[g * HD:(g + 1) * HD, :])
        if d == nwb:
            win_block()
        else:
            pl.when(qb - nwb + d >= 0)(win_block)

    for pair in range(N_HEADS // 2):
        halves = []
        for h in (2 * pair, 2 * pair + 1):
            g, i = divmod(h, I)
            cols = slice(i * QB, (i + 1) * QB)
            o_s = acc_scr[g, :, cols] / l_scr[g, 0:1, cols]
            o_w = acc_scr[G + g, :, cols] / l_scr[G + g, 0:1, cols]
            o_c = oc_scr[g, :, cols]
            halves.append(gates_t[3 * h:3 * h + 1, :] * o_c + gates_t[3 * h + 1:3 * h + 2, :] * o_s + gates_t[3 * h + 2:3 * h + 3, :] * o_w)
        o_ref[:, pair * LANE:(pair + 1) * LANE] = jnp.concatenate(halves, axis=0).T


def nsa_prompt_attn(B, S, proj, kc_s, vc_st):
    nq = S // Q_BLOCK
    ncb = S // CMP_STRIDE
    nsb = S // SLC_BLOCK
    tok = lambda cb: (lambda b, q: (b * nq + q, cb))
    seq = lambda cb: (lambda b, q: (b, cb))
    one = pl.Buffered(1)
    kvc = COL_KV // LANE
    w4 = HEADS_PER_GROUP * Q_BLOCK
    return pl.pallas_call(
        functools.partial(_nsa_prompt_kernel, S=S),
        grid=(B, nq),
        in_specs=[pl.BlockSpec((Q_BLOCK, N_HEADS * LANE), tok(COL_Q // (N_HEADS * LANE))),
                  pl.BlockSpec((Q_BLOCK, LANE), tok(COL_SMALL // LANE)),
                  pl.BlockSpec((1, ncb, LANE), lambda b, q: (b, 0, 0)),
                  pl.BlockSpec((1, LANE, ncb), lambda b, q: (b, 0, 0)),
                  pl.BlockSpec((S, LANE), seq(kvc + 2), pipeline_mode=one),
                  pl.BlockSpec((S, LANE), seq(kvc + 3), pipeline_mode=one),
                  pl.BlockSpec((S, LANE), seq(kvc + 4), pipeline_mode=one),
                  pl.BlockSpec((S, LANE), seq(kvc + 5), pipeline_mode=one)],
        out_specs=pl.BlockSpec((Q_BLOCK, D_B), tok(0)),
        out_shape=jax.ShapeDtypeStruct((B * S, D_B), F32),
        scratch_shapes=[pltpu.VMEM((nq, Q_BLOCK, LANE), BF16), pltpu.VMEM((nq, LANE, Q_BLOCK), BF16),
                        pltpu.VMEM((nq, Q_BLOCK, LANE), BF16), pltpu.VMEM((nq, LANE, Q_BLOCK), BF16),
                        pltpu.VMEM((KV_GROUPS, LANE, w4), BF16),
                        pltpu.VMEM((N_HEADS, Q_BLOCK, Q_BLOCK), F32),
                        pltpu.VMEM((ncb + 16, Q_BLOCK), F32),
                        pltpu.VMEM((KV_GROUPS, nsb, Q_BLOCK), F32),
                        pltpu.VMEM((2 * KV_GROUPS, 8, w4), F32),
                        pltpu.VMEM((2 * KV_GROUPS, 8, w4), F32),
                        pltpu.VMEM((2 * KV_GROUPS, HEAD_DIM, w4), F32),
                        pltpu.VMEM((KV_GROUPS, HEAD_DIM, w4), F32)],
        compiler_params=pltpu.CompilerParams(dimension_semantics=("arbitrary", "arbitrary"), vmem_limit_bytes=VMEM_LIMIT),
        name="nsa_prompt",
    )(proj, proj, kc_s, vc_st, proj, proj, proj, proj)


N_PAGES = PAST_LEN // PAGE_SIZE
S_NCB = PAST_LEN // CMP_STRIDE
S_NSB = (PAST_LEN + SLC_BLOCK) // SLC_BLOCK
S_NSB_PAD = -(-S_NSB // 8) * 8
PAGES_PER_STEP = 16
CMP_STEPS = N_PAGES // PAGES_PER_STEP
NCOL = DEC_SEQ * N_HEADS
N_DYN = N_SELECT - 3
DYN_ROWS = N_DYN * SLC_BLOCK
ROW_DYN0 = 2 * SLC_BLOCK
ROW_NEW = ROW_DYN0 + DEC_SEQ * KV_GROUPS * DYN_ROWS
NEW_PAD = LANE
ROWS_ALL = ROW_NEW + NEW_PAD
WIN_BUF = min(WINDOW, PAST_LEN)
WIN_ROWS = WIN_BUF + NEW_PAD
assert ROWS_ALL % LANE == 0 and WIN_ROWS % LANE == 0
assert (PAST_LEN + DEC_SEQ - 1) // SLC_BLOCK == S_NSB - 1 and PAST_LEN % SLC_BLOCK == 0 and DEC_SEQ <= 8
assert CMP_STRIDE * S_NCB + CMP_BLOCK - 1 > PAST_LEN + DEC_SEQ - 1


def _col_consts():
    c = np.arange(LANE)
    t = c // N_HEADS
    h = c % N_HEADS
    ok = c < NCOL
    info = np.zeros((8, LANE), np.float32)
    info[0] = np.where(ok, 2.0 ** (-8.0 * (h + 1) / N_HEADS), 0.0)
    info[1] = np.where(ok, PAST_LEN + t, 0)
    info[2] = t
    info[3] = ok
    return info, t, h // HEADS_PER_GROUP, ok


def _sel_mask():
    _, t, g, ok = _col_consts()
    m = np.zeros((ROWS_ALL, LANE), np.float32)
    m[0:ROW_DYN0] = ok[None, :]
    for tt in range(DEC_SEQ):
        for gg in range(KV_GROUPS):
            r0 = ROW_DYN0 + (tt * KV_GROUPS + gg) * DYN_ROWS
            m[r0:r0 + DYN_ROWS] = (ok & (t == tt) & (g == gg))[None, :]
    r = np.arange(NEW_PAD)
    m[ROW_NEW:ROW_NEW + NEW_PAD] = (ok[None, :] & (r[:, None] <= t[None, :]) & (r[:, None] < DEC_SEQ))
    return m


def _compress_paged_kernel(pt_ref, *refs):
    P = PAGES_PER_STEP
    k_pages, k_next = refs[0:P], refs[P]
    v_pages, v_next = refs[P + 1:2 * P + 1], refs[2 * P + 1]
    wk_ref, wv_ref, pk_ref, pv_ref, kc_ref, vct_ref = refs[2 * P + 2:]
    n = P * PAGE_SIZE // CMP_STRIDE
    not_last = (pl.program_id(1) < pl.num_programs(1) - 1).astype(F32)
    last_row = lax.broadcasted_iota(jnp.int32, (n, LANE), 0) == n - 1

    def pooled(pages, nxt, w_ref):
        x = jnp.concatenate([p[0, 0].T for p in pages], axis=0)
        tail_next = jnp.sum(nxt[0, 0].T[0:CMP_STRIDE, :] * w_ref[CMP_STRIDE:CMP_BLOCK, :], axis=0, keepdims=True) * not_last
        return _pool(x, w_ref, n) + jnp.where(last_row, tail_next, 0.0)

    kc_ref[0] = _mm(pooled(k_pages, k_next, wk_ref).astype(BF16), pk_ref[...]).astype(BF16)
    vct_ref[0] = _mm(pooled(v_pages, v_next, wv_ref).astype(BF16), pv_ref[...]).T.astype(BF16)


def compress_paged(l, pool_k, pool_v, page_table, cw):
    B = page_table.shape[0]
    P = PAGES_PER_STEP
    wk, wv, pk, pv = cw
    n = P * PAGE_SIZE // CMP_STRIDE
    page = lambda i: pl.BlockSpec((1, 1, PAGE_SIZE, LANE), lambda b, j, pt: (l, pt[b, j * P + i], 0, 0))
    nxt = pl.BlockSpec((1, 1, PAGE_SIZE, LANE), lambda b, j, pt: (l, pt[b, jnp.minimum(j * P + P, N_PAGES - 1)], 0, 0))
    const = lambda b, j, pt: (0, 0)
    wspecs = [pl.BlockSpec((CMP_BLOCK, LANE), const), pl.BlockSpec((CMP_BLOCK, LANE), const),
              pl.BlockSpec((LANE, LANE), const), pl.BlockSpec((LANE, LANE), const)]
    gs = pltpu.PrefetchScalarGridSpec(
        num_scalar_prefetch=1, grid=(B, CMP_STEPS),
        in_specs=[page(i) for i in range(P)] + [nxt] + [page(i) for i in range(P)] + [nxt] + wspecs,
        out_specs=[pl.BlockSpec((1, n, LANE), lambda b, j, pt: (b, j, 0)), pl.BlockSpec((1, LANE, n), lambda b, j, pt: (b, 0, j))])
    return pl.pallas_call(
        _compress_paged_kernel, grid_spec=gs,
        out_shape=[jax.ShapeDtypeStruct((B, S_NCB, LANE), BF16), jax.ShapeDtypeStruct((B, LANE, S_NCB), BF16)],
        compiler_params=pltpu.CompilerParams(dimension_semantics=("arbitrary", "arbitrary"), vmem_limit_bytes=VMEM_LIMIT),
        name="compress_paged",
    )(page_table, *([pool_k] * (P + 1)), *([pool_v] * (P + 1)), wk, wv, pk, pv)


def _query_cols(q32):
    qp = jnp.concatenate([q32, jnp.zeros((LANE - NCOL, LANE), F32)], axis=0)
    return (qp * Q_SCALE).T.astype(BF16)


def _own_group_lanes(o):
    lane_g = lax.broadcasted_iota(jnp.int32, (NCOL, LANE), 1) // HEAD_DIM
    row_g = (lax.broadcasted_iota(jnp.int32, (NCOL, LANE), 0) % N_HEADS) // HEADS_PER_GROUP
    return jnp.where(lane_g == row_g, o, 0.0)


def _sample_cmp_kernel(q_ref, kc_ref, vct_ref, info_ref, grp_ref, oc_ref, idx_ref, p_scr):
    w = _query_cols(q_ref[0])
    slope = info_ref[0:1, :]
    qpos = info_ref[1:2, :]
    n_f = lax.broadcasted_iota(jnp.int32, (S_NCB, LANE), 0).astype(F32)
    dist = qpos - (CMP_STRIDE * n_f + (CMP_BLOCK - 1))
    mask = dist >= 0.0
    s = jnp.where(mask, _mm(kc_ref[0], w) - slope * dist, NEG)
    e = jnp.exp(s - jnp.max(s, axis=0, keepdims=True))
    pn = jnp.where(mask, e, 0.0) / jnp.sum(e, axis=0, keepdims=True)
    oc_ref[0] = _own_group_lanes(_mm(vct_ref[0], pn.astype(BF16)).T[0:NCOL, :])
    pg = jnp.dot(pn, grp_ref[...], preferred_element_type=F32, precision=HI)
    p_scr[0:8, :] = jnp.zeros((8, LANE), F32)
    p_scr[8:8 + S_NCB, :] = pg
    p_scr[8 + S_NCB:, :] = jnp.zeros((p_scr.shape[0] - 8 - S_NCB, LANE), F32)
    imp = p_scr[pl.ds(7, S_NSB_PAD, stride=4), :]
    for d in range(1, 5):
        imp = imp + p_scr[pl.ds(7 + d, S_NSB_PAD, stride=4), :]
    j_i = lax.broadcasted_iota(jnp.int32, (S_NSB_PAD, LANE), 0)
    tq = PAST_LEN + lax.broadcasted_iota(jnp.int32, (S_NSB_PAD, LANE), 1) // KV_GROUPS
    cur = lax.shift_right_arithmetic(tq, 6)
    forced = (j_i == 0) | (j_i == cur) | (j_i == cur - 1)
    valid = j_i * SLC_BLOCK <= tq
    score = jnp.where(j_i >= S_NSB, REMOVED, jnp.where(forced, BIG, jnp.where(valid, imp, NEG)))
    jf = j_i.astype(F32)
    picks = []
    for _ in range(N_SELECT):
        cm = jnp.max(score, axis=0, keepdims=True)
        idx = jnp.min(jnp.where(score == cm, jf, 1e9), axis=0, keepdims=True)
        score = jnp.where(jf == idx, REMOVED, score)
        picks.append(idx)
    idx_ref[0] = jnp.concatenate(picks, axis=0).astype(jnp.int32)


def sample_cmp(q32, kc_s, vc_st):
    B = q32.shape[0]
    info, t, g, ok = _col_consts()
    grp = np.zeros((LANE, LANE), np.float32)
    for c in range(NCOL):
        grp[c, t[c] * KV_GROUPS + g[c]] = 1.0
    per_b = lambda b: (b, 0, 0)
    const = lambda b: (0, 0)
    return pl.pallas_call(
        _sample_cmp_kernel, grid=(B,),
        in_specs=[pl.BlockSpec((1, NCOL, LANE), per_b), pl.BlockSpec((1, S_NCB, LANE), per_b), pl.BlockSpec((1, LANE, S_NCB), per_b),
                  pl.BlockSpec((8, LANE), const), pl.BlockSpec((LANE, LANE), const)],
        out_specs=[pl.BlockSpec((1, NCOL, LANE), per_b), pl.BlockSpec((1, N_SELECT, LANE), per_b)],
        out_shape=[jax.ShapeDtypeStruct((B, NCOL, LANE), F32), jax.ShapeDtypeStruct((B, N_SELECT, LANE), jnp.int32)],
        scratch_shapes=[pltpu.VMEM((8 + 4 * S_NSB_PAD + 8, LANE), F32)],
        compiler_params=pltpu.CompilerParams(dimension_semantics=("arbitrary",), vmem_limit_bytes=VMEM_LIMIT),
        name="sample_cmp",
    )(q32, kc_s, vc_st, jnp.asarray(info), jnp.asarray(grp))


def _sample_gather_kernel(pt_ref, idx_ref, *refs):
    k_pg, v_pg = refs[0:N_DYN], refs[N_DYN:2 * N_DYN]
    kd_ref, vd_ref, tk, tv = refs[2 * N_DYN:]
    b = pl.program_id(0)
    tg = pl.program_id(1)
    ntg = DEC_SEQ * KV_GROUPS
    for k in range(N_DYN):
        blk = jnp.minimum(idx_ref[b, (k + 3) * ntg + tg], S_NSB - 2)
        r0 = pl.multiple_of((blk % (PAGE_SIZE // SLC_BLOCK)) * SLC_BLOCK, SLC_BLOCK)
        tk[...] = k_pg[k][0, 0].T
        tv[...] = v_pg[k][0, 0].T
        rows = slice(k * SLC_BLOCK, (k + 1) * SLC_BLOCK)
        kd_ref[0, 0, rows, :] = tk[pl.ds(r0, SLC_BLOCK), :].astype(BF16)
        vd_ref[0, 0, rows, :] = tv[pl.ds(r0, SLC_BLOCK), :].astype(BF16)


def sample_gather(l, pool_k, pool_v, page_table, idx):
    B = page_table.shape[0]
    ntg = DEC_SEQ * KV_GROUPS
    per_page = PAGE_SIZE // SLC_BLOCK

    def dyn(k):
        def im(b, tg, pt, ix):
            blk = jnp.minimum(ix[b, (k + 3) * ntg + tg], S_NSB - 2)
            return (l, pt[b, blk // per_page], 0, 0)
        return pl.BlockSpec((1, 1, LANE, PAGE_SIZE), im)
    specs = [dyn(k) for k in range(N_DYN)]
    out = pl.BlockSpec((1, 1, DYN_ROWS, LANE), lambda b, tg, pt, ix: (b, tg, 0, 0))
    gs = pltpu.PrefetchScalarGridSpec(num_scalar_prefetch=2, grid=(B, ntg), in_specs=specs + specs, out_specs=[out, out],
                                      scratch_shapes=[pltpu.VMEM((PAGE_SIZE, LANE), F32), pltpu.VMEM((PAGE_SIZE, LANE), F32)])
    shape = jax.ShapeDtypeStruct((B, ntg, DYN_ROWS, LANE), BF16)
    return pl.pallas_call(
        _sample_gather_kernel, grid_spec=gs, out_shape=[shape, shape],
        compiler_params=pltpu.CompilerParams(dimension_semantics=("arbitrary", "arbitrary"), vmem_limit_bytes=VMEM_LIMIT),
        name="sample_gather",
    )(page_table, idx, *([pool_k] * N_DYN), *([pool_v] * N_DYN))


def _sample_sel_kernel(pt_ref, idx_ref, kd_ref, vd_ref, k0_ref, k1_ref, v0_ref, v1_ref, knew_ref, vnew_ref, kwb_ref, vwb_ref,
                       kwn_ref, vwn_ref, q_ref, gate_ref, oc_ref, info_ref, mask_ref, o_ref, kpos):
    b = pl.program_id(0)
    w = _query_cols(q_ref[0])
    slope = info_ref[0:1, :]
    qpos = info_ref[1:2, :]
    tcol = info_ref[2:3, :]
    okc = info_ref[3:4, :]
    r_f = lax.broadcasted_iota(jnp.int32, (NEW_PAD, LANE), 0).astype(F32)
    cur = (PAST_LEN + DEC_SEQ - 1) // SLC_BLOCK
    ntg = DEC_SEQ * KV_GROUPS
    zpad = jnp.zeros((NEW_PAD - 8, LANE), F32)

    kpos[0:SLC_BLOCK, :] = r_f[0:SLC_BLOCK, :]
    kpos[SLC_BLOCK:ROW_DYN0, :] = float((cur - 1) * SLC_BLOCK) + r_f[0:SLC_BLOCK, :]
    for tg in range(ntg):
        for k in range(N_DYN):
            r0 = ROW_DYN0 + (tg * N_DYN + k) * SLC_BLOCK
            kpos[r0:r0 + SLC_BLOCK, :] = (idx_ref[b, (k + 3) * ntg + tg] * SLC_BLOCK).astype(F32) + r_f[0:SLC_BLOCK, :]
    kpos[ROW_NEW:ROWS_ALL, :] = float(cur * SLC_BLOCK) + r_f

    def gathered(first, prev, dyn_ref, new_ref):
        parts = [first[0, 0].T[0:SLC_BLOCK, :].astype(BF16), prev[0, 0].T[PAGE_SIZE - SLC_BLOCK:, :].astype(BF16)]
        parts += [dyn_ref[0, tg] for tg in range(ntg)]
        return jnp.concatenate(parts + [jnp.concatenate([new_ref[0], zpad], axis=0).astype(BF16)], axis=0)

    def attend(k_rows, v_rows, mask, dist):
        s = jnp.where(mask, _mm(k_rows, w) - slope * dist, NEG)
        p = jnp.exp(s - jnp.max(s, axis=0, keepdims=True))
        l = jnp.sum(p, axis=0, keepdims=True)
        o_t = lax.dot_general(v_rows, p.astype(BF16), (((0,), (0,)), ((), ())), preferred_element_type=F32)
        return (o_t / l).T[0:NCOL, :]

    o_s = attend(gathered(k0_ref, k1_ref, kd_ref, knew_ref), gathered(v0_ref, v1_ref, vd_ref, vnew_ref),
                 mask_ref[...] > 0.5, qpos - kpos[...])
    i_w = lax.broadcasted_iota(jnp.int32, (WIN_ROWS, LANE), 0).astype(F32)
    mask_w = (i_w >= tcol) & (i_w <= tcol + float(WIN_BUF)) & (okc > 0.5)
    kw_rows = jnp.concatenate([kwb_ref[0, 0].T, kwn_ref[0], zpad], axis=0).astype(BF16)
    vw_rows = jnp.concatenate([vwb_ref[0, 0].T, vwn_ref[0], zpad], axis=0).astype(BF16)
    o_w = attend(kw_rows, vw_rows, mask_w, float(WIN_BUF) + tcol - i_w)
    gates = jax.nn.sigmoid(gate_ref[0])
    o_ref[0] = _own_group_lanes(gates[:, 0:1] * oc_ref[0] + gates[:, 1:2] * o_s + gates[:, 2:3] * o_w)


def sample_sel_dense(l, pool_k, pool_v, page_table, idx, knew, vnew, kwin, vwin, kwnew, vwnew, q32, gate32, oc):
    B = page_table.shape[0]
    info, _, _, _ = _col_consts()
    ntg = DEC_SEQ * KV_GROUPS
    per_page = PAGE_SIZE // SLC_BLOCK
    kd, vd = sample_gather(l, pool_k, pool_v, page_table, idx)
    per_b = lambda b, pt, ix: (b, 0, 0)
    const = lambda b, pt, ix: (0, 0)
    dense = pl.BlockSpec((1, ntg, DYN_ROWS, LANE), lambda b, pt, ix: (b, 0, 0, 0))
    first = pl.BlockSpec((1, 1, LANE, PAGE_SIZE), lambda b, pt, ix: (l, pt[b, 0], 0, 0))
    prev = pl.BlockSpec((1, 1, LANE, PAGE_SIZE), lambda b, pt, ix: (l, pt[b, N_PAGES - 1], 0, 0))
    new8 = pl.BlockSpec((1, 8, LANE), per_b)
    win = pl.BlockSpec((1, 1, LANE, WIN_BUF), lambda b, pt, ix: (l, b, 0, 0))
    col32 = pl.BlockSpec((1, NCOL, LANE), per_b)
    gs = pltpu.PrefetchScalarGridSpec(
        num_scalar_prefetch=2, grid=(B,),
        in_specs=[dense, dense, first, prev, first, prev, new8, new8, win, win, new8, new8, col32, col32, col32,
                  pl.BlockSpec((8, LANE), const), pl.BlockSpec((ROWS_ALL, LANE), const)],
        out_specs=col32,
        scratch_shapes=[pltpu.VMEM((ROWS_ALL, LANE), F32)])
    return pl.pallas_call(
        _sample_sel_kernel, grid_spec=gs,
        out_shape=jax.ShapeDtypeStruct((B, NCOL, LANE), F32),
        compiler_params=pltpu.CompilerParams(dimension_semantics=("arbitrary",), vmem_limit_bytes=VMEM_LIMIT),
        name="sample_sel",
    )(page_table, idx, kd, vd, pool_k, pool_k, pool_v, pool_v, knew, vnew, kwin, vwin, kwnew, vwnew,
      q32, gate32, oc, jnp.asarray(info), jnp.asarray(_sel_mask()))


def nsa_sample_attn(l, proj, pools, win, page_table, cw):
    B = page_table.shape[0]
    T = DEC_SEQ
    pool_kc, pool_vc, pool_ks, pool_vs = pools
    n_pool = pool_kc.shape[1]
    pages = lambda p: p.transpose(0, 1, 3, 4, 2).reshape(DEPTH, n_pool, LANE, PAGE_SIZE)
    kc_s, vc_st = compress_paged(l, pages(pool_kc), pages(pool_vc), page_table, cw)
    q32 = proj[:, COL_Q:COL_Q + N_HEADS * LANE].reshape(B, NCOL, LANE)
    oc, idx = sample_cmp(q32, kc_s, vc_st)
    idx = idx[:, :, :T * KV_GROUPS].reshape(B, N_SELECT * T * KV_GROUPS)

    def new8(c):
        return jnp.pad(proj[:, c:c + LANE].reshape(B, T, LANE), ((0, 0), (0, 8 - T), (0, 0)))
    small = proj[:, COL_SMALL:COL_SMALL + LANE].reshape(B, T, LANE)
    gate32 = jnp.pad(small[:, :, SMALL_G:SMALL_G + 3 * N_HEADS].reshape(B, NCOL, 3), ((0, 0), (0, 0), (0, LANE - 3)))
    wins = [w.transpose(0, 1, 3, 4, 2).reshape(DEPTH, B, LANE, WIN_BUF) for w in win]
    o32 = sample_sel_dense(l, pages(pool_ks), pages(pool_vs), page_table, idx, new8(COL_KV + 2 * LANE), new8(COL_KV + 3 * LANE),
                     wins[0], wins[1], new8(COL_KV + 4 * LANE), new8(COL_KV + 5 * LANE), q32, gate32, oc)
    o4 = o32.reshape(B, T, N_HEADS, KV_GROUPS, HEAD_DIM)
    o = jnp.concatenate([o4[:, :, :HEADS_PER_GROUP, 0], o4[:, :, HEADS_PER_GROUP:, 1]], axis=2)
    return o.reshape(B * T, D_B)


def _layer(x, B, T, lw, state, cmp, sample_ctx, g_final, final_norm):
    proj = rms_proj(x, lw["g_mix"], lw["w_in"], lw["b_in"])

    def piece(c0, width):
        return proj[:, c0:c0 + width].reshape(B, T, width)

    m_qk, m_v, m_o = piece(COL_QK, 2 * D_A), piece(COL_V, D_A), piece(COL_O, D_A)
    small = piece(COL_SMALL, LANE)
    kv = [piece(COL_KV + i * KV_W, KV_W).reshape(B, T, KV_GROUPS, HEAD_DIM) for i in range(6)]
    kc, vc, ks, vs, kw, vw = kv

    conv_buf, C, n, m = state
    conv_new = jnp.concatenate([conv_buf, m_qk], axis=1)[:, T:]
    margs = (conv_buf, C, n, m, lw["conv_w"], lw["conv_b"], lw["m_norm"])
    if sample_ctx is None:
        h_a, C1, n1, m1 = mlstm(B, T, proj, COL_QK // (2 * D_A), proj, COL_V // D_A, proj, COL_O // D_A, proj, COL_SMALL // LANE, *margs)
    else:
        def pad_tok(a, fill):
            return jnp.concatenate([a, jnp.broadcast_to(fill, (B, M_CHUNK - T, a.shape[-1]))], axis=1).reshape(B * M_CHUNK, a.shape[-1])
        fill_small = jnp.zeros((LANE,), F32).at[SMALL_I:SMALL_I + M_HEADS].set(NEG).at[SMALL_F:SMALL_F + M_HEADS].set(BIG)
        zero = jnp.zeros((1,), F32)
        h_pad, C1, n1, m1 = mlstm(B, M_CHUNK, pad_tok(m_qk, zero), 0, pad_tok(m_v, zero), 0, pad_tok(m_o, zero), 0, pad_tok(small, fill_small), 0, *margs)
        h_a = h_pad.reshape(B, M_CHUNK, D_A)[:, :T].reshape(B * T, D_A)
    cw = _cmp_weights(*cmp)
    if sample_ctx is None:
        kvc = COL_KV // LANE
        kc_s, vc_st = compress_prompt(B, T, proj, kvc, proj, kvc + 1, cw)
        o_b = nsa_prompt_attn(B, T, proj, kc_s, vc_st)
        keep = min(WINDOW, T)
        kw_state, vw_state = kw[:, T - keep:], vw[:, T - keep:]
    else:
        l, pools, win, page_table = sample_ctx
        o_b = nsa_sample_attn(l, proj, pools, win, page_table, cw)
        keep = min(WINDOW, WIN_BUF + T)
        kw_state = jnp.concatenate([win[0][l], kw], axis=1)[:, WIN_BUF + T - keep:]
        vw_state = jnp.concatenate([win[1][l], vw], axis=1)[:, WIN_BUF + T - keep:]

    x = merge(x, h_a.reshape(B * T, D_A), o_b.reshape(B * T, D_B), proj, lw["w_a"], lw["w_b"], lw["w_o"])
    x = ffn(x, lw["g_ffn"], lw["w_ffn_in"], lw["w_ffn_out"], g_final, final_norm)
    return x, (kc, vc, ks, vs, kw_state, vw_state, conv_new, C1, n1, m1)


def kernel(x_prompt, x_sample, cache_k_cmp, cache_v_cmp, cache_k_slc, cache_v_slc, state_k_win, state_v_win, state_conv, state_C, state_n, state_m, page_table, norm_mix, w_in, b_in, conv_w, conv_b, cmp_pos_k, cmp_pos_v, cmp_proj_k, cmp_proj_v, m_norm, w_a, w_b, w_o, norm_ffn, w_ffn_in, w_ffn_out, norm_final):
    yp = x_prompt.reshape(BATCH * SEQ, D_MODEL)
    ys = x_sample.reshape(DEC_BATCH * DEC_SEQ, D_MODEL)
    p_states, s_states = [], []
    for l in range(DEPTH):
        w_in_l, b_in_l = _prep_w_in(w_in[l], b_in[l])
        lw = dict(g_mix=norm_mix[l], w_in=w_in_l, b_in=b_in_l, conv_w=conv_w[l], conv_b=conv_b[l], m_norm=m_norm[l],
                  w_a=w_a[l].astype(BF16), w_b=w_b[l].astype(BF16), w_o=w_o[l].astype(BF16), g_ffn=norm_ffn[l],
                  w_ffn_in=w_ffn_in[l].astype(BF16), w_ffn_out=w_ffn_out[l].astype(BF16))
        cmp = (cmp_pos_k[l], cmp_pos_v[l], cmp_proj_k[l], cmp_proj_v[l])
        last = l == DEPTH - 1
        conv0 = jnp.zeros((BATCH, CONV_W - 1, 2 * D_A), F32)
        C0 = jnp.zeros((BATCH, M_HEADS, M_HEAD_DIM, M_HEAD_DIM), F32)
        n0 = jnp.zeros((BATCH, M_HEADS, M_HEAD_DIM), F32)
        m0 = jnp.full((BATCH, M_HEADS), NEG, F32)
        yp, st_p = _layer(yp, BATCH, SEQ, lw, (conv0, C0, n0, m0), cmp, None, norm_final, last)
        p_states.append(st_p)
        ctx = (l, (cache_k_cmp, cache_v_cmp, cache_k_slc, cache_v_slc), (state_k_win, state_v_win), page_table)
        ys, st_s = _layer(ys, DEC_BATCH, DEC_SEQ, lw, (state_conv[l], state_C[l], state_n[l], state_m[l]), cmp, ctx, norm_final, last)
        s_states.append(st_s)
    y_prompt = yp.reshape(BATCH, SEQ, D_MODEL)
    y_sample = ys.reshape(DEC_BATCH, DEC_SEQ, D_MODEL)
    P = [jnp.stack(a) for a in zip(*p_states)]
    S = [jnp.stack(a) for a in zip(*s_states)]
    return (y_prompt, y_sample, P[0], S[0], P[1], S[1], P[2], S[2], P[3], S[3], P[4], S[4],
            P[5], S[5], P[6], S[6], P[7], S[7], P[8], S[8], P[9], S[9])
```

```python
import functools
import jax, jax.numpy as jnp
from jax import lax
import numpy as np
from jax.experimental import pallas as pl
from jax.experimental.pallas import tpu as pltpu

D_MODEL = 1024
BATCH = 2
SEQ = 8192
DEPTH = 2
DEC_BATCH = 32
DEC_SEQ = 4
PAST_LEN = 16384
PAGE_SIZE = 128
M_HEADS = 4
M_HEAD_DIM = D_MODEL // 8
D_A = M_HEADS * M_HEAD_DIM
CONV_W = 4
M_CHUNK = 128
N_HEADS = 8
HEAD_DIM = D_MODEL // 16
D_B = N_HEADS * HEAD_DIM
KV_GROUPS = 2
HEADS_PER_GROUP = N_HEADS // KV_GROUPS
KV_W = KV_GROUPS * HEAD_DIM
CMP_STRIDE = 16
CMP_BLOCK = 2 * CMP_STRIDE
SLC_BLOCK = 64
N_SELECT = 16
WINDOW = 512
Q_BLOCK = 128
D_FF = -(-8 * D_MODEL // (3 * 256)) * 256
RMS_EPS = 1e-6
NEG = -1e30
BIG = 1e30
SIZES = (2 * D_A, D_A, D_A, M_HEADS, M_HEADS, D_B, KV_W, KV_W, KV_W, KV_W, KV_W, KV_W, 3 * N_HEADS, 2 * D_MODEL)
SPLITS = tuple(int(s) for s in np.cumsum(SIZES)[:-1])
D_IN = sum(SIZES)

F32 = jnp.float32
BF16 = jnp.bfloat16
LANE = 128
VMEM_LIMIT = 48 * 1024 * 1024

COL_QK = 0
COL_GMA = 1024
COL_GMB = 2048
COL_V = 3072
COL_O = 3584
COL_Q = 4096
COL_KV = 5120
COL_SMALL = 5888
N_PROJ = 6144
SMALL_I = 0
SMALL_F = M_HEADS
SMALL_G = 2 * M_HEADS


def _prep_w_in(w, b):
    def cols(a):
        (m_qk, m_v, m_o, m_i, m_f, a_q, a_kc, a_vc, a_ks, a_vs, a_kw, a_vw, a_g, g_mrg) = jnp.split(a, SPLITS, axis=-1)
        z64 = jnp.zeros(a.shape[:-1] + (HEAD_DIM,), a.dtype)
        slabs = []
        for h in range(N_HEADS):
            qh = a_q[..., h * HEAD_DIM:(h + 1) * HEAD_DIM]
            slabs += [qh, z64] if h < HEADS_PER_GROUP else [z64, qh]
        small = jnp.concatenate([m_i, m_f, a_g, jnp.zeros(a.shape[:-1] + (LANE - 2 * M_HEADS - 3 * N_HEADS,), a.dtype)], -1)
        pad = jnp.zeros(a.shape[:-1] + (N_PROJ - COL_SMALL - LANE,), a.dtype)
        return jnp.concatenate([m_qk, g_mrg, m_v, m_o] + slabs + [a_kc, a_vc, a_ks, a_vs, a_kw, a_vw, small, pad], -1)
    return cols(w).astype(BF16), cols(b[None, :])


def _rms_proj_kernel(x_ref, g_ref, w_ref, b_ref, o_ref, h_scr):
    @pl.when(pl.program_id(1) == 0)
    def _():
        x = x_ref[...]
        ms = jnp.mean(x * x, axis=-1, keepdims=True)
        h_scr[...] = (x * lax.rsqrt(ms + RMS_EPS) * g_ref[...]).astype(BF16)
    o_ref[...] = jnp.dot(h_scr[...], w_ref[...], preferred_element_type=F32) + b_ref[...]


def rms_proj(x, g, w, b):
    T, D = x.shape
    N = w.shape[1]
    tm = min(T, 1024)
    tn = 2048
    return pl.pallas_call(
        _rms_proj_kernel,
        grid=(T // tm, N // tn),
        in_specs=[pl.BlockSpec((tm, D), lambda i, j: (i, 0)),
                  pl.BlockSpec((1, D), lambda i, j: (0, 0)),
                  pl.BlockSpec((D, tn), lambda i, j: (0, j)),
                  pl.BlockSpec((1, tn), lambda i, j: (0, j))],
        out_specs=pl.BlockSpec((tm, tn), lambda i, j: (i, j)),
        out_shape=jax.ShapeDtypeStruct((T, N), F32),
        scratch_shapes=[pltpu.VMEM((tm, D), BF16)],
        compiler_params=pltpu.CompilerParams(dimension_semantics=("arbitrary", "arbitrary"), vmem_limit_bytes=VMEM_LIMIT),
        name="rms_proj",
    )(x, g.reshape(1, D), w, b)


def _merge_kernel(x_ref, ha_ref, ob_ref, ga_ref, gb_ref, wa_ref, wb_ref, wo_ref, o_ref):
    ya = jnp.dot(ha_ref[...].astype(BF16), wa_ref[...], preferred_element_type=F32)
    yb = jnp.dot(ob_ref[...].astype(BF16), wb_ref[...], preferred_element_type=F32)
    y = jax.nn.sigmoid(ga_ref[...]) * ya + jax.nn.sigmoid(gb_ref[...]) * yb
    o_ref[...] = x_ref[...] + jnp.dot(y.astype(BF16), wo_ref[...], preferred_element_type=F32)


def merge(x, ha, ob, proj, wa, wb, wo):
    T, D = x.shape
    tm = min(T, 512)
    row = lambda i: (i, 0)
    const = lambda i: (0, 0)
    return pl.pallas_call(
        _merge_kernel,
        grid=(T // tm,),
        in_specs=[pl.BlockSpec((tm, D), row),
                  pl.BlockSpec((tm, D_A), row),
                  pl.BlockSpec((tm, D_B), row),
                  pl.BlockSpec((tm, D), lambda i: (i, COL_GMA // D_MODEL)),
                  pl.BlockSpec((tm, D), lambda i: (i, COL_GMB // D_MODEL)),
                  pl.BlockSpec((D_A, D), const),
                  pl.BlockSpec((D_B, D), const),
                  pl.BlockSpec((D, D), const)],
        out_specs=pl.BlockSpec((tm, D), row),
        out_shape=jax.ShapeDtypeStruct((T, D), F32),
        compiler_params=pltpu.CompilerParams(dimension_semantics=("arbitrary",), vmem_limit_bytes=VMEM_LIMIT),
        name="merge",
    )(x, ha, ob, proj, proj, wa, wb, wo)


def _ffn_kernel(x_ref, g_ref, wg_ref, wu_ref, wo_ref, gf_ref, o_ref, h_scr, acc_scr, *, final_norm):
    j = pl.program_id(1)

    @pl.when(j == 0)
    def _():
        x = x_ref[...]
        ms = jnp.mean(x * x, axis=-1, keepdims=True)
        h_scr[...] = (x * lax.rsqrt(ms + RMS_EPS) * g_ref[...]).astype(BF16)
        acc_scr[...] = jnp.zeros_like(acc_scr)

    h = h_scr[...]
    gate = jnp.dot(h, wg_ref[...], preferred_element_type=F32)
    up = jnp.dot(h, wu_ref[...], preferred_element_type=F32)
    act = (gate * jax.nn.sigmoid(gate) * up).astype(BF16)
    acc_scr[...] += jnp.dot(act, wo_ref[...], preferred_element_type=F32)

    @pl.when(j == pl.num_programs(1) - 1)
    def _():
        y = x_ref[...] + acc_scr[...]
        if final_norm:
            ms = jnp.mean(y * y, axis=-1, keepdims=True)
            y = y * lax.rsqrt(ms + RMS_EPS) * gf_ref[...]
        o_ref[...] = y


def ffn(x, g, w_in, w_out, g_final, final_norm):
    T, D = x.shape
    tm = min(T, 512)
    tf = D_FF // 2
    nf = D_FF // tf
    return pl.pallas_call(
        functools.partial(_ffn_kernel, final_norm=final_norm),
        grid=(T // tm, nf),
        in_specs=[pl.BlockSpec((tm, D), lambda i, j: (i, 0)),
                  pl.BlockSpec((1, D), lambda i, j: (0, 0)),
                  pl.BlockSpec((D, tf), lambda i, j: (0, j)),
                  pl.BlockSpec((D, tf), lambda i, j: (0, j + nf)),
                  pl.BlockSpec((tf, D), lambda i, j: (j, 0)),
                  pl.BlockSpec((1, D), lambda i, j: (0, 0))],
        out_specs=pl.BlockSpec((tm, D), lambda i, j: (i, 0)),
        out_shape=jax.ShapeDtypeStruct((T, D), F32),
        scratch_shapes=[pltpu.VMEM((tm, D), BF16), pltpu.VMEM((tm, D), F32)],
        compiler_params=pltpu.CompilerParams(dimension_semantics=("arbitrary", "arbitrary"), vmem_limit_bytes=VMEM_LIMIT),
        name="ffn",
    )(x, g.reshape(1, D), w_in, w_in, w_out, g_final.reshape(1, D))


HI = lax.Precision.HIGHEST


def _mm(a, b):
    return jnp.dot(a, b, preferred_element_type=F32)


def _log_sigmoid(x):
    return jnp.minimum(x, 0.0) - jnp.log1p(jnp.exp(-jnp.abs(x)))


def _mlstm_kernel(qk_ref, v_ref, o_ref, small_ref, cbuf_ref, c0_ref, n0_ref, m0_ref, cw_ref, cb_ref, g_ref,
                  h_ref, c_out_ref, n_out_ref, m_out_ref, ext_scr, c_scr, n_scr, m_scr):
    L = M_CHUNK
    c = pl.program_id(1)

    @pl.when(c == 0)
    def _():
        ext_scr[0:8, :] = cbuf_ref[0]
        c_scr[...] = c0_ref[0]
        n_scr[...] = n0_ref[0]
        m_scr[...] = m0_ref[0]

    ext_scr[8:8 + L, :] = qk_ref[...]
    conv = cb_ref[...] + ext_scr[5:5 + L, :] * cw_ref[0:1, :]
    for j in range(1, CONV_W):
        conv = conv + ext_scr[5 + j:5 + j + L, :] * cw_ref[j:j + 1, :]
    ext_scr[0:8, :] = qk_ref[L - 8:L, :]
    qk = conv * jax.nn.sigmoid(conv)

    small = small_ref[...]
    small_t = small.T
    row = lax.broadcasted_iota(jnp.int32, (L, L), 0)
    col = lax.broadcasted_iota(jnp.int32, (L, L), 1)
    causal = col <= row
    fcol_all = jnp.dot(causal.astype(F32), _log_sigmoid(small), preferred_element_type=F32, precision=HI)
    frow_all = jnp.dot(_log_sigmoid(small_t), (row <= col).astype(F32), preferred_element_type=F32, precision=HI)

    for h in range(M_HEADS):
        sl = slice(h * M_HEAD_DIM, (h + 1) * M_HEAD_DIM)
        q = qk[:, sl]
        k = qk[:, D_A + h * M_HEAD_DIM:D_A + (h + 1) * M_HEAD_DIM] * (M_HEAD_DIM ** -0.5)
        fc = fcol_all[:, SMALL_F + h:SMALL_F + h + 1]
        fr = frow_all[SMALL_F + h:SMALL_F + h + 1, :]
        igr = small_t[SMALL_I + h:SMALL_I + h + 1, :]
        igc = small[:, SMALL_I + h:SMALL_I + h + 1]
        m_prev = m_scr[h:h + 1, 0:1]
        d = jnp.where(causal, fc - fr + igr, NEG)
        a = fc + m_prev
        m_t = jnp.maximum(a, jnp.max(d, axis=-1, keepdims=True))
        qb = q.astype(BF16)
        kb = k.astype(BF16)
        vb = v_ref[:, sl].astype(BF16)
        s = lax.dot_general(qb, kb, (((1,), (1,)), ((), ())), preferred_element_type=F32) * jnp.exp(d - m_t)
        inter = jnp.exp(a - m_t)
        num = _mm(s.astype(BF16), vb) + inter * _mm(qb, c_scr[h].astype(BF16))
        den = jnp.sum(s, axis=-1, keepdims=True) + inter * jnp.sum(q * n_scr[h:h + 1, :], axis=-1, keepdims=True)
        hh = num / jnp.maximum(jnp.abs(den), jnp.exp(-m_t))
        mu = jnp.mean(hh, axis=-1, keepdims=True)
        var = jnp.mean(jnp.square(hh - mu), axis=-1, keepdims=True)
        hn = (hh - mu) * lax.rsqrt(var + RMS_EPS) * g_ref[:, sl]
        h_ref[:, sl] = jax.nn.sigmoid(o_ref[:, sl]) * hn
        f_l = fr[:, L - 1:L]
        m_new = jnp.maximum(f_l + m_prev, jnp.max(f_l - fr + igr, axis=-1, keepdims=True))
        decay = jnp.exp(f_l + m_prev - m_new)
        ks = k * jnp.exp(f_l - fc + igc - m_new)
        c_scr[h] = decay * c_scr[h] + lax.dot_general(ks.astype(BF16), vb, (((0,), (0,)), ((), ())), preferred_element_type=F32)
        n_scr[h:h + 1, :] = decay * n_scr[h:h + 1, :] + jnp.sum(ks, axis=0, keepdims=True)
        m_scr[h:h + 1, :] = jnp.broadcast_to(m_new, (1, LANE))

    @pl.when(c == pl.num_programs(1) - 1)
    def _():
        c_out_ref[0] = c_scr[...]
        n_out_ref[0] = n_scr[...]
        m_out_ref[0] = m_scr[...]


def mlstm(B, T, qk, qk_col, v, v_col, o, o_col, small, small_col, conv_buf, C0, n0, m0, conv_w, conv_b, head_g):
    L = M_CHUNK
    nch = T // L
    cbuf = jnp.concatenate([jnp.zeros((B, 8 - (CONV_W - 1), 2 * D_A), F32), conv_buf], axis=1)
    n0p = jnp.concatenate([n0, jnp.zeros((B, 8 - M_HEADS, M_HEAD_DIM), F32)], axis=1)
    m0p = jnp.broadcast_to(jnp.concatenate([m0, jnp.zeros((B, 8 - M_HEADS), F32)], axis=1)[:, :, None], (B, 8, LANE))
    tok = lambda cb: (lambda b, c: (b * nch + c, cb))
    per_b3 = lambda b, c: (b, 0, 0)
    per_b4 = lambda b, c: (b, 0, 0, 0)
    const = lambda b, c: (0, 0)
    h, C1, n1, m1 = pl.pallas_call(
        _mlstm_kernel,
        grid=(B, nch),
        in_specs=[pl.BlockSpec((L, 2 * D_A), tok(qk_col)),
                  pl.BlockSpec((L, D_A), tok(v_col)),
                  pl.BlockSpec((L, D_A), tok(o_col)),
                  pl.BlockSpec((L, LANE), tok(small_col)),
                  pl.BlockSpec((1, 8, 2 * D_A), per_b3),
                  pl.BlockSpec((1, M_HEADS, M_HEAD_DIM, M_HEAD_DIM), per_b4),
                  pl.BlockSpec((1, 8, M_HEAD_DIM), per_b3),
                  pl.BlockSpec((1, 8, LANE), per_b3),
                  pl.BlockSpec((CONV_W, 2 * D_A), const),
                  pl.BlockSpec((1, 2 * D_A), const),
                  pl.BlockSpec((1, D_A), const)],
        out_specs=[pl.BlockSpec((L, D_A), tok(0)),
                   pl.BlockSpec((1, M_HEADS, M_HEAD_DIM, M_HEAD_DIM), per_b4),
                   pl.BlockSpec((1, 8, M_HEAD_DIM), per_b3),
                   pl.BlockSpec((1, 8, LANE), per_b3)],
        out_shape=[jax.ShapeDtypeStruct((B * T, D_A), F32),
                   jax.ShapeDtypeStruct((B, M_HEADS, M_HEAD_DIM, M_HEAD_DIM), F32),
                   jax.ShapeDtypeStruct((B, 8, M_HEAD_DIM), F32),
                   jax.ShapeDtypeStruct((B, 8, LANE), F32)],
        scratch_shapes=[pltpu.VMEM((8 + L, 2 * D_A), F32),
                        pltpu.VMEM((M_HEADS, M_HEAD_DIM, M_HEAD_DIM), F32),
                        pltpu.VMEM((8, M_HEAD_DIM), F32),
                        pltpu.VMEM((8, LANE), F32)],
        compiler_params=pltpu.CompilerParams(dimension_semantics=("arbitrary", "arbitrary"), vmem_limit_bytes=VMEM_LIMIT),
        name="mlstm",
    )(qk, v, o, small, cbuf, C0, n0p, m0p, conv_w, conv_b.reshape(1, -1), head_g.reshape(1, -1))
    return h, C1, n1[:, :M_HEADS], m1[:, :M_HEADS, 0]


SLOPES = tuple(float(2.0 ** (-8.0 * (h + 1) / N_HEADS)) for h in range(N_HEADS))
Q_SCALE = HEAD_DIM ** -0.5
REMOVED = -3.0e38
SEL_CHAIN = 4


def _pool(x, w_ref, n):
    x3 = x.reshape(n, CMP_STRIDE, LANE)
    head = jnp.sum(x3 * w_ref[0:CMP_STRIDE, :][None], axis=1)
    tail = jnp.sum(x3 * w_ref[CMP_STRIDE:CMP_BLOCK, :][None], axis=1)
    nxt = pltpu.roll(tail, n - 1, 0)
    last = lax.broadcasted_iota(jnp.int32, (n, LANE), 0) == n - 1
    return head + jnp.where(last, 0.0, nxt)


def _compress_kernel(k_ref, v_ref, wk_ref, wv_ref, pk_ref, pv_ref, kc_ref, vct_ref, *, n):
    pk = _pool(k_ref[...], wk_ref, n)
    pv = _pool(v_ref[...], wv_ref, n)
    kc_ref[0] = _mm(pk.astype(BF16), pk_ref[...]).astype(BF16)
    vct_ref[0] = _mm(pv.astype(BF16), pv_ref[...]).T.astype(BF16)


def _cmp_weights(pos_k, pos_v, proj_k, proj_v):
    tile = lambda w: jnp.concatenate([w] * KV_GROUPS, axis=1)
    bd = lambda w: jnp.kron(jnp.eye(KV_GROUPS, dtype=F32), w).astype(BF16)
    return tile(pos_k), tile(pos_v), bd(proj_k), bd(proj_v)


def compress_prompt(B, S, kc, kc_col, vc, vc_col, cw):
    n = S // CMP_STRIDE
    wk, wv, pk, pv = cw
    const = lambda b: (0, 0)
    return pl.pallas_call(
        functools.partial(_compress_kernel, n=n),
        grid=(B,),
        in_specs=[pl.BlockSpec((S, LANE), lambda b: (b, kc_col)),
                  pl.BlockSpec((S, LANE), lambda b: (b, vc_col)),
                  pl.BlockSpec((CMP_BLOCK, LANE), const), pl.BlockSpec((CMP_BLOCK, LANE), const),
                  pl.BlockSpec((LANE, LANE), const), pl.BlockSpec((LANE, LANE), const)],
        out_specs=[pl.BlockSpec((1, n, LANE), lambda b: (b, 0, 0)), pl.BlockSpec((1, LANE, n), lambda b: (b, 0, 0))],
        out_shape=[jax.ShapeDtypeStruct((B, n, LANE), BF16), jax.ShapeDtypeStruct((B, LANE, n), BF16)],
        compiler_params=pltpu.CompilerParams(dimension_semantics=("arbitrary",), vmem_limit_bytes=VMEM_LIMIT),
        name="compress_prompt",
    )(kc, vc, wk, wv, pk, pv)


def _nsa_prompt_kernel(q_ref, small_ref, kc_ref, vct_ref, ks_ref, vs_ref, kw_ref, vw_ref, o_ref,
                       ksb, vstb, kwb, vwtb, qt_scr, sd_scr, p_scr, sel_scr, m_scr, l_scr, acc_scr, oc_scr, *, S):
    QB = Q_BLOCK
    nkb = S // QB
    assert nkb <= 64
    ncb = S // CMP_STRIDE
    nsb = S // SLC_BLOCK
    G, I, HD = KV_GROUPS, HEADS_PER_GROUP, HEAD_DIM
    qb = pl.program_id(1)
    c_i = lax.broadcasted_iota(jnp.int32, (QB, QB), 0)
    r_i = lax.broadcasted_iota(jnp.int32, (QB, QB), 1)

    @pl.when(qb == 0)
    def _():
        def cp(i, carry):
            rows = pl.ds(pl.multiple_of(i * QB, QB), QB)
            ksb[i] = ks_ref[rows, :].astype(BF16)
            kwb[i] = kw_ref[rows, :].astype(BF16)
            vstb[i] = vs_ref[rows, :].T.astype(BF16)
            vwtb[i] = vw_ref[rows, :].T.astype(BF16)
            return carry
        lax.fori_loop(0, nkb, cp, 0)
        rc = (r_i - c_i).astype(F32)
        for h in range(N_HEADS):
            sd_scr[h] = SLOPES[h] * rc
        p_scr[0:8, :] = jnp.zeros((8, LANE), F32)
        p_scr[8 + ncb:16 + ncb, :] = jnp.zeros((8, LANE), F32)

    for h in range(N_HEADS):
        g, i = divmod(h, I)
        qt_scr[g, :, i * QB:(i + 1) * QB] = (q_ref[:, h * LANE:(h + 1) * LANE] * Q_SCALE).T.astype(BF16)
    small_t = small_ref[...].T
    gates_t = jax.nn.sigmoid(small_t[SMALL_G:SMALL_G + 3 * N_HEADS, :])
    m_scr[...] = jnp.full(m_scr.shape, NEG, F32)
    l_scr[...] = jnp.zeros(l_scr.shape, F32)
    acc_scr[...] = jnp.zeros(acc_scr.shape, F32)

    n_i = lax.broadcasted_iota(jnp.int32, (ncb, QB), 0)
    rr_i = lax.broadcasted_iota(jnp.int32, (ncb, QB), 1)
    dist_c = (rr_i - CMP_STRIDE * n_i - (CMP_BLOCK - 1) + QB * qb).astype(F32)
    mask_c = dist_c >= 0.0
    j_i = lax.broadcasted_iota(jnp.int32, (nsb, QB), 0)
    tq = lax.broadcasted_iota(jnp.int32, (nsb, QB), 1) + QB * qb
    cur = lax.shift_right_arithmetic(tq, 6)
    forced = (j_i == 0) | (j_i == cur) | (j_i == cur - 1)
    valid = j_i * SLC_BLOCK <= tq
    jf = j_i.astype(F32)
    jb_i = lax.broadcasted_iota(jnp.int32, (nsb, 1), 0)
    need_words = []
    for g in range(G):
        s4 = _mm(kc_ref[0], qt_scr[g])
        psum = jnp.zeros((ncb, QB), F32)
        pns = []
        for i in range(I):
            h = g * I + i
            s = jnp.where(mask_c, s4[:, i * QB:(i + 1) * QB] - SLOPES[h] * dist_c, NEG)
            e = jnp.exp(s - jnp.max(s, axis=0, keepdims=True))
            pn = jnp.where(mask_c, e, 0.0) / jnp.sum(e, axis=0, keepdims=True)
            psum = psum + pn
            pns.append(pn.astype(BF16))
        oc_scr[g] = _mm(vct_ref[0][g * HD:(g + 1) * HD, :], jnp.concatenate(pns, axis=1))
        p_scr[8:8 + ncb, :] = psum
        imp = p_scr[pl.ds(7, nsb, stride=4), :]
        for d in range(1, 5):
            imp = imp + p_scr[pl.ds(7 + d, nsb, stride=4), :]
        score = jnp.where(forced, BIG, jnp.where(valid, imp, NEG))
        sel = jnp.zeros((nsb, QB), F32)
        for _ in range(min(N_SELECT, nsb)):
            cm = jnp.max(score, axis=0, keepdims=True)
            idx = jnp.min(jnp.where(score == cm, jf, 1e9), axis=0, keepdims=True)
            hit = jf == idx
            sel = jnp.where(hit, 1.0, sel)
            score = jnp.where(hit, REMOVED, score)
        sel_scr[g] = sel
        used = jnp.max(sel, axis=1, keepdims=True) > 0.5
        kb_j = lax.shift_right_logical(jb_i, 1)
        bits = jnp.where(used, lax.shift_left(jnp.ones_like(jb_i), kb_j & 31), 0)

        def word(half, parity):
            return jnp.sum(jnp.where((lax.shift_right_logical(kb_j, 5) == half) & ((jb_i & 1) == parity), bits, 0))
        need_words.append((word(0, 0) | word(0, 1), word(1, 0) | word(1, 1)))

    def sel_chain(kbs, g, diag_last):
        krows = jnp.concatenate([ksb[k] for k in kbs], axis=0)
        vcols = jnp.concatenate([vstb[k, g * HD:(g + 1) * HD, :] for k in kbs], axis=1)
        masks, offs = [], []
        for j, k in enumerate(kbs):
            m0 = sel_scr[g, pl.ds(2 * k, 1), :]
            m1 = sel_scr[g, pl.ds(2 * k + 1, 1), :]
            mk = jnp.concatenate([jnp.broadcast_to(m0, (SLC_BLOCK, QB)), jnp.broadcast_to(m1, (SLC_BLOCK, QB))], axis=0) > 0.5
            if diag_last and j == len(kbs) - 1:
                mk = mk & (c_i <= r_i)
            masks.append(mk)
            offs.append((qb - k).astype(F32) * float(QB))
        m_old = m_scr[g, 0:1, :]
        l_old = l_scr[g, 0:1, :]
        s4 = _mm(krows, qt_scr[g])
        ps, m_new, l_new, alphas = [], [], [], []
        for i in range(I):
            h = g * I + i
            cols = slice(i * QB, (i + 1) * QB)
            s = jnp.concatenate([jnp.where(masks[j], s4[j * QB:(j + 1) * QB, cols] - sd_scr[h] - SLOPES[h] * offs[j], NEG)
                                 for j in range(len(kbs))], axis=0)
            mo = m_old[:, cols]
            mn = jnp.maximum(mo, jnp.max(s, axis=0, keepdims=True))
            p = jnp.exp(s - mn)
            al = jnp.exp(mo - mn)
            ps.append(p.astype(BF16))
            m_new.append(mn)
            alphas.append(al)
            l_new.append(al * l_old[:, cols] + jnp.sum(p, axis=0, keepdims=True))
        acc_scr[g] = jnp.concatenate(alphas, axis=1) * acc_scr[g] + _mm(vcols, jnp.concatenate(ps, axis=1))
        m_scr[g, 0:1, :] = jnp.concatenate(m_new, axis=1)
        l_scr[g, 0:1, :] = jnp.concatenate(l_new, axis=1)

    none = jnp.int32(-1)
    for g in range(G):
        lo, hi = need_words[g]

        def sel_body(kb, pend, g=g, lo=lo, hi=hi):
            bit = (lax.shift_right_logical(jnp.where(kb < 32, lo, hi), kb & 31) & 1) != 0
            full = pend[-1] >= 0
            pl.when(jnp.logical_and(bit, full))(lambda: sel_chain(list(pend) + [kb], g, False))
            new = []
            for j, pj in enumerate(pend):
                first_free = (pj < 0) if j == 0 else jnp.logical_and(pend[j - 1] >= 0, pj < 0)
                new.append(jnp.where(bit, jnp.where(full, none, jnp.where(first_free, kb, pj)), pj))
            return tuple(new)
        pend = lax.fori_loop(0, qb, sel_body, (none,) * (SEL_CHAIN - 1))
        for n in range(SEL_CHAIN):
            has_n = (pend[0] < 0) if n == 0 else (pend[n - 1] >= 0)
            if n < SEL_CHAIN - 1:
                has_n = jnp.logical_and(has_n, pend[n] < 0) if n > 0 else has_n
            pl.when(has_n)(lambda n=n, pend=pend, g=g: sel_chain(list(pend[:n]) + [qb], g, True))

    nwb = WINDOW // QB
    kbs = [jnp.maximum(qb - nwb + d, 0) for d in range(nwb + 1)]
    kw_rows = jnp.concatenate([kwb[k] for k in kbs], axis=0)
    vw_cols = jnp.concatenate([vwtb[k] for k in kbs], axis=1)
    win_masks = []
    for d in range(nwb + 1):
        inside = (jnp.zeros_like(c_i) + (qb - nwb + d)) >= 0
        win_masks.append(inside & (r_i <= c_i) if d == 0 else ((c_i <= r_i) if d == nwb else inside))
    for g in range(G):
        s4 = _mm(kw_rows, qt_scr[g])
        ps, ls = [], []
        for i in range(I):
            h = g * I + i
            s = jnp.concatenate([jnp.where(win_masks[d], s4[d * QB:(d + 1) * QB, i * QB:(i + 1) * QB]
                                           - (sd_scr[h] + SLOPES[h] * float((nwb - d) * QB)), NEG)
                                 for d in range(nwb + 1)], axis=0)
            p = jnp.exp(s - jnp.max(s, axis=0, keepdims=True))
            ls.append(jnp.sum(p, axis=0, keepdims=True))
            ps.append(p.astype(BF16))
        acc_scr[G + g] = _mm(vw_cols[g * HD:(g + 1) * HD, :], jnp.concatenate(ps, axis=1))
        l_scr[G + g, 0:1, :] = jnp.concatenate(ls, axis=1)

    for pair in range(N_HEADS // 2):
        halves = []
        for h in (2 * pair, 2 * pair + 1):
            g, i = divmod(h, I)
            cols = slice(i * QB, (i + 1) * QB)
            o_s = acc_scr[g, :, cols] / l_scr[g, 0:1, cols]
            o_w = acc_scr[G + g, :, cols] / l_scr[G + g, 0:1, cols]
            o_c = oc_scr[g, :, cols]
            halves.append(gates_t[3 * h:3 * h + 1, :] * o_c + gates_t[3 * h + 1:3 * h + 2, :] * o_s + gates_t[3 * h + 2:3 * h + 3, :] * o_w)
        o_ref[:, pair * LANE:(pair + 1) * LANE] = jnp.concatenate(halves, axis=0).T


def nsa_prompt_attn(B, S, proj, kc_s, vc_st):
    nq = S // Q_BLOCK
    ncb = S // CMP_STRIDE
    nsb = S // SLC_BLOCK
    tok = lambda cb: (lambda b, q: (b * nq + q, cb))
    seq = lambda cb: (lambda b, q: (b, cb))
    one = pl.Buffered(1)
    kvc = COL_KV // LANE
    w4 = HEADS_PER_GROUP * Q_BLOCK
    return pl.pallas_call(
        functools.partial(_nsa_prompt_kernel, S=S),
        grid=(B, nq),
        in_specs=[pl.BlockSpec((Q_BLOCK, N_HEADS * LANE), tok(COL_Q // (N_HEADS * LANE))),
                  pl.BlockSpec((Q_BLOCK, LANE), tok(COL_SMALL // LANE)),
                  pl.BlockSpec((1, ncb, LANE), lambda b, q: (b, 0, 0)),
                  pl.BlockSpec((1, LANE, ncb), lambda b, q: (b, 0, 0)),
                  pl.BlockSpec((S, LANE), seq(kvc + 2), pipeline_mode=one),
                  pl.BlockSpec((S, LANE), seq(kvc + 3), pipeline_mode=one),
                  pl.BlockSpec((S, LANE), seq(kvc + 4), pipeline_mode=one),
                  pl.BlockSpec((S, LANE), seq(kvc + 5), pipeline_mode=one)],
        out_specs=pl.BlockSpec((Q_BLOCK, D_B), tok(0)),
        out_shape=jax.ShapeDtypeStruct((B * S, D_B), F32),
        scratch_shapes=[pltpu.VMEM((nq, Q_BLOCK, LANE), BF16), pltpu.VMEM((nq, LANE, Q_BLOCK), BF16),
                        pltpu.VMEM((nq, Q_BLOCK, LANE), BF16), pltpu.VMEM((nq, LANE, Q_BLOCK), BF16),
                        pltpu.VMEM((KV_GROUPS, LANE, w4), BF16),
                        pltpu.VMEM((N_HEADS, Q_BLOCK, Q_BLOCK), F32),
                        pltpu.VMEM((ncb + 16, Q_BLOCK), F32),
                        pltpu.VMEM((KV_GROUPS, nsb, Q_BLOCK), F32),
                        pltpu.VMEM((2 * KV_GROUPS, 8, w4), F32),
                        pltpu.VMEM((2 * KV_GROUPS, 8, w4), F32),
                        pltpu.VMEM((2 * KV_GROUPS, HEAD_DIM, w4), F32),
                        pltpu.VMEM((KV_GROUPS, HEAD_DIM, w4), F32)],
        compiler_params=pltpu.CompilerParams(dimension_semantics=("arbitrary", "arbitrary"), vmem_limit_bytes=VMEM_LIMIT),
        name="nsa_prompt",
    )(proj, proj, kc_s, vc_st, proj, proj, proj, proj)


N_PAGES = PAST_LEN // PAGE_SIZE
S_NCB = PAST_LEN // CMP_STRIDE
S_NSB = (PAST_LEN + SLC_BLOCK) // SLC_BLOCK
S_NSB_PAD = -(-S_NSB // 8) * 8
PAGES_PER_STEP = 16
CMP_STEPS = N_PAGES // PAGES_PER_STEP
NCOL = DEC_SEQ * N_HEADS
N_DYN = N_SELECT - 3
DYN_ROWS = N_DYN * SLC_BLOCK
ROW_DYN0 = 2 * SLC_BLOCK
ROW_NEW = ROW_DYN0 + DEC_SEQ * KV_GROUPS * DYN_ROWS
NEW_PAD = LANE
ROWS_ALL = ROW_NEW + NEW_PAD
WIN_BUF = min(WINDOW, PAST_LEN)
WIN_ROWS = WIN_BUF + NEW_PAD
assert ROWS_ALL % LANE == 0 and WIN_ROWS % LANE == 0
assert (PAST_LEN + DEC_SEQ - 1) // SLC_BLOCK == S_NSB - 1 and PAST_LEN % SLC_BLOCK == 0 and DEC_SEQ <= 8
assert CMP_STRIDE * S_NCB + CMP_BLOCK - 1 > PAST_LEN + DEC_SEQ - 1


def _col_consts():
    c = np.arange(LANE)
    t = c // N_HEADS
    h = c % N_HEADS
    ok = c < NCOL
    info = np.zeros((8, LANE), np.float32)
    info[0] = np.where(ok, 2.0 ** (-8.0 * (h + 1) / N_HEADS), 0.0)
    info[1] = np.where(ok, PAST_LEN + t, 0)
    info[2] = t
    info[3] = ok
    return info, t, h // HEADS_PER_GROUP, ok


def _sel_mask():
    _, t, g, ok = _col_consts()
    m = np.zeros((ROWS_ALL, LANE), np.float32)
    m[0:ROW_DYN0] = ok[None, :]
    for tt in range(DEC_SEQ):
        for gg in range(KV_GROUPS):
            r0 = ROW_DYN0 + (tt * KV_GROUPS + gg) * DYN_ROWS
            m[r0:r0 + DYN_ROWS] = (ok & (t == tt) & (g == gg))[None, :]
    r = np.arange(NEW_PAD)
    m[ROW_NEW:ROW_NEW + NEW_PAD] = (ok[None, :] & (r[:, None] <= t[None, :]) & (r[:, None] < DEC_SEQ))
    return m


def _compress_paged_kernel(pt_ref, *refs):
    P = PAGES_PER_STEP
    k_pages, k_next = refs[0:P], refs[P]
    v_pages, v_next = refs[P + 1:2 * P + 1], refs[2 * P + 1]
    wk_ref, wv_ref, pk_ref, pv_ref, kc_ref, vct_ref = refs[2 * P + 2:]
    n = P * PAGE_SIZE // CMP_STRIDE
    not_last = (pl.program_id(1) < pl.num_programs(1) - 1).astype(F32)
    last_row = lax.broadcasted_iota(jnp.int32, (n, LANE), 0) == n - 1

    def pooled(pages, nxt, w_ref):
        x = jnp.concatenate([p[0, 0].T for p in pages], axis=0)
        tail_next = jnp.sum(nxt[0, 0].T[0:CMP_STRIDE, :] * w_ref[CMP_STRIDE:CMP_BLOCK, :], axis=0, keepdims=True) * not_last
        return _pool(x, w_ref, n) + jnp.where(last_row, tail_next, 0.0)

    kc_ref[0] = _mm(pooled(k_pages, k_next, wk_ref).astype(BF16), pk_ref[...]).astype(BF16)
    vct_ref[0] = _mm(pooled(v_pages, v_next, wv_ref).astype(BF16), pv_ref[...]).T.astype(BF16)


def compress_paged(l, pool_k, pool_v, page_table, cw):
    B = page_table.shape[0]
    P = PAGES_PER_STEP
    wk, wv, pk, pv = cw
    n = P * PAGE_SIZE // CMP_STRIDE
    page = lambda i: pl.BlockSpec((1, 1, PAGE_SIZE, LANE), lambda b, j, pt: (l, pt[b, j * P + i], 0, 0))
    nxt = pl.BlockSpec((1, 1, PAGE_SIZE, LANE), lambda b, j, pt: (l, pt[b, jnp.minimum(j * P + P, N_PAGES - 1)], 0, 0))
    const = lambda b, j, pt: (0, 0)
    wspecs = [pl.BlockSpec((CMP_BLOCK, LANE), const), pl.BlockSpec((CMP_BLOCK, LANE), const),
              pl.BlockSpec((LANE, LANE), const), pl.BlockSpec((LANE, LANE), const)]
    gs = pltpu.PrefetchScalarGridSpec(
        num_scalar_prefetch=1, grid=(B, CMP_STEPS),
        in_specs=[page(i) for i in range(P)] + [nxt] + [page(i) for i in range(P)] + [nxt] + wspecs,
        out_specs=[pl.BlockSpec((1, n, LANE), lambda b, j, pt: (b, j, 0)), pl.BlockSpec((1, LANE, n), lambda b, j, pt: (b, 0, j))])
    return pl.pallas_call(
        _compress_paged_kernel, grid_spec=gs,
        out_shape=[jax.ShapeDtypeStruct((B, S_NCB, LANE), BF16), jax.ShapeDtypeStruct((B, LANE, S_NCB), BF16)],
        compiler_params=pltpu.CompilerParams(dimension_semantics=("arbitrary", "arbitrary"), vmem_limit_bytes=VMEM_LIMIT),
        name="compress_paged",
    )(page_table, *([pool_k] * (P + 1)), *([pool_v] * (P + 1)), wk, wv, pk, pv)


def _query_cols(q32):
    qp = jnp.concatenate([q32, jnp.zeros((LANE - NCOL, LANE), F32)], axis=0)
    return (qp * Q_SCALE).T.astype(BF16)


def _own_group_lanes(o):
    lane_g = lax.broadcasted_iota(jnp.int32, (NCOL, LANE), 1) // HEAD_DIM
    row_g = (lax.broadcasted_iota(jnp.int32, (NCOL, LANE), 0) % N_HEADS) // HEADS_PER_GROUP
    return jnp.where(lane_g == row_g, o, 0.0)


def _sample_cmp_kernel(q_ref, kc_ref, vct_ref, info_ref, grp_ref, oc_ref, idx_ref, p_scr):
    w = _query_cols(q_ref[0])
    slope = info_ref[0:1, :]
    qpos = info_ref[1:2, :]
    n_f = lax.broadcasted_iota(jnp.int32, (S_NCB, LANE), 0).astype(F32)
    dist = qpos - (CMP_STRIDE * n_f + (CMP_BLOCK - 1))
    mask = dist >= 0.0
    s = jnp.where(mask, _mm(kc_ref[0], w) - slope * dist, NEG)
    e = jnp.exp(s - jnp.max(s, axis=0, keepdims=True))
    pn = jnp.where(mask, e, 0.0) / jnp.sum(e, axis=0, keepdims=True)
    oc_ref[0] = _own_group_lanes(_mm(vct_ref[0], pn.astype(BF16)).T[0:NCOL, :])
    pg = jnp.dot(pn, grp_ref[...], preferred_element_type=F32, precision=HI)
    p_scr[0:8, :] = jnp.zeros((8, LANE), F32)
    p_scr[8:8 + S_NCB, :] = pg
    p_scr[8 + S_NCB:, :] = jnp.zeros((p_scr.shape[0] - 8 - S_NCB, LANE), F32)
    imp = p_scr[pl.ds(7, S_NSB_PAD, stride=4), :]
    for d in range(1, 5):
        imp = imp + p_scr[pl.ds(7 + d, S_NSB_PAD, stride=4), :]
    j_i = lax.broadcasted_iota(jnp.int32, (S_NSB_PAD, LANE), 0)
    tq = PAST_LEN + lax.broadcasted_iota(jnp.int32, (S_NSB_PAD, LANE), 1) // KV_GROUPS
    cur = lax.shift_right_arithmetic(tq, 6)
    forced = (j_i == 0) | (j_i == cur) | (j_i == cur - 1)
    valid = j_i * SLC_BLOCK <= tq
    score = jnp.where(j_i >= S_NSB, REMOVED, jnp.where(forced, BIG, jnp.where(valid, imp, NEG)))
    jf = j_i.astype(F32)
    picks = []
    for _ in range(N_SELECT):
        cm = jnp.max(score, axis=0, keepdims=True)
        idx = jnp.min(jnp.where(score == cm, jf, 1e9), axis=0, keepdims=True)
        score = jnp.where(jf == idx, REMOVED, score)
        picks.append(idx)
    idx_ref[0] = jnp.concatenate(picks, axis=0).astype(jnp.int32)


def sample_cmp(q32, kc_s, vc_st):
    B = q32.shape[0]
    info, t, g, ok = _col_consts()
    grp = np.zeros((LANE, LANE), np.float32)
    for c in range(NCOL):
        grp[c, t[c] * KV_GROUPS + g[c]] = 1.0
    per_b = lambda b: (b, 0, 0)
    const = lambda b: (0, 0)
    return pl.pallas_call(
        _sample_cmp_kernel, grid=(B,),
        in_specs=[pl.BlockSpec((1, NCOL, LANE), per_b), pl.BlockSpec((1, S_NCB, LANE), per_b), pl.BlockSpec((1, LANE, S_NCB), per_b),
                  pl.BlockSpec((8, LANE), const), pl.BlockSpec((LANE, LANE), const)],
        out_specs=[pl.BlockSpec((1, NCOL, LANE), per_b), pl.BlockSpec((1, N_SELECT, LANE), per_b)],
        out_shape=[jax.ShapeDtypeStruct((B, NCOL, LANE), F32), jax.ShapeDtypeStruct((B, N_SELECT, LANE), jnp.int32)],
        scratch_shapes=[pltpu.VMEM((8 + 4 * S_NSB_PAD + 8, LANE), F32)],
        compiler_params=pltpu.CompilerParams(dimension_semantics=("arbitrary",), vmem_limit_bytes=VMEM_LIMIT),
        name="sample_cmp",
    )(q32, kc_s, vc_st, jnp.asarray(info), jnp.asarray(grp))


def _sample_gather_kernel(pt_ref, idx_ref, *refs):
    k_pg, v_pg = refs[0:N_DYN], refs[N_DYN:2 * N_DYN]
    kd_ref, vd_ref, tk, tv = refs[2 * N_DYN:]
    b = pl.program_id(0)
    tg = pl.program_id(1)
    ntg = DEC_SEQ * KV_GROUPS
    for k in range(N_DYN):
        blk = jnp.minimum(idx_ref[b, (k + 3) * ntg + tg], S_NSB - 2)
        r0 = pl.multiple_of((blk % (PAGE_SIZE // SLC_BLOCK)) * SLC_BLOCK, SLC_BLOCK)
        tk[...] = k_pg[k][0, 0].T
        tv[...] = v_pg[k][0, 0].T
        rows = slice(k * SLC_BLOCK, (k + 1) * SLC_BLOCK)
        kd_ref[0, 0, rows, :] = tk[pl.ds(r0, SLC_BLOCK), :].astype(BF16)
        vd_ref[0, 0, rows, :] = tv[pl.ds(r0, SLC_BLOCK), :].astype(BF16)


def sample_gather(l, pool_k, pool_v, page_table, idx):
    B = page_table.shape[0]
    ntg = DEC_SEQ * KV_GROUPS
    per_page = PAGE_SIZE // SLC_BLOCK

    def dyn(k):
        def im(b, tg, pt, ix):
            blk = jnp.minimum(ix[b, (k + 3) * ntg + tg], S_NSB - 2)
            return (l, pt[b, blk // per_page], 0, 0)
        return pl.BlockSpec((1, 1, LANE, PAGE_SIZE), im)
    specs = [dyn(k) for k in range(N_DYN)]
    out = pl.BlockSpec((1, 1, DYN_ROWS, LANE), lambda b, tg, pt, ix: (b, tg, 0, 0))
    gs = pltpu.PrefetchScalarGridSpec(num_scalar_prefetch=2, grid=(B, ntg), in_specs=specs + specs, out_specs=[out, out],
                                      scratch_shapes=[pltpu.VMEM((PAGE_SIZE, LANE), F32), pltpu.VMEM((PAGE_SIZE, LANE), F32)])
    shape = jax.ShapeDtypeStruct((B, ntg, DYN_ROWS, LANE), BF16)
    return pl.pallas_call(
        _sample_gather_kernel, grid_spec=gs, out_shape=[shape, shape],
        compiler_params=pltpu.CompilerParams(dimension_semantics=("arbitrary", "arbitrary"), vmem_limit_bytes=VMEM_LIMIT),
        name="sample_gather",
    )(page_table, idx, *([pool_k] * N_DYN), *([pool_v] * N_DYN))


def _sample_sel_kernel(pt_ref, idx_ref, kd_ref, vd_ref, k0_ref, k1_ref, v0_ref, v1_ref, knew_ref, vnew_ref, kwb_ref, vwb_ref,
                       kwn_ref, vwn_ref, q_ref, gate_ref, oc_ref, info_ref, mask_ref, o_ref, kpos):
    b = pl.program_id(0)
    w = _query_cols(q_ref[0])
    slope = info_ref[0:1, :]
    qpos = info_ref[1:2, :]
    tcol = info_ref[2:3, :]
    okc = info_ref[3:4, :]
    r_f = lax.broadcasted_iota(jnp.int32, (NEW_PAD, LANE), 0).astype(F32)
    cur = (PAST_LEN + DEC_SEQ - 1) // SLC_BLOCK
    ntg = DEC_SEQ * KV_GROUPS
    zpad = jnp.zeros((NEW_PAD - 8, LANE), F32)

    kpos[0:SLC_BLOCK, :] = r_f[0:SLC_BLOCK, :]
    kpos[SLC_BLOCK:ROW_DYN0, :] = float((cur - 1) * SLC_BLOCK) + r_f[0:SLC_BLOCK, :]
    for tg in range(ntg):
        for k in range(N_DYN):
            r0 = ROW_DYN0 + (tg * N_DYN + k) * SLC_BLOCK
            kpos[r0:r0 + SLC_BLOCK, :] = (idx_ref[b, (k + 3) * ntg + tg] * SLC_BLOCK).astype(F32) + r_f[0:SLC_BLOCK, :]
    kpos[ROW_NEW:ROWS_ALL, :] = float(cur * SLC_BLOCK) + r_f

    def gathered(first, prev, dyn_ref, new_ref):
        parts = [first[0, 0].T[0:SLC_BLOCK, :].astype(BF16), prev[0, 0].T[PAGE_SIZE - SLC_BLOCK:, :].astype(BF16)]
        parts += [dyn_ref[0, tg] for tg in range(ntg)]
        return jnp.concatenate(parts + [jnp.concatenate([new_ref[0], zpad], axis=0).astype(BF16)], axis=0)

    def attend(k_rows, v_rows, mask, dist):
        s = jnp.where(mask, _mm(k_rows, w) - slope * dist, NEG)
        p = jnp.exp(s - jnp.max(s, axis=0, keepdims=True))
        l = jnp.sum(p, axis=0, keepdims=True)
        o_t = lax.dot_general(v_rows, p.astype(BF16), (((0,), (0,)), ((), ())), preferred_element_type=F32)
        return (o_t / l).T[0:NCOL, :]

    o_s = attend(gathered(k0_ref, k1_ref, kd_ref, knew_ref), gathered(v0_ref, v1_ref, vd_ref, vnew_ref),
                 mask_ref[...] > 0.5, qpos - kpos[...])
    i_w = lax.broadcasted_iota(jnp.int32, (WIN_ROWS, LANE), 0).astype(F32)
    mask_w = (i_w >= tcol) & (i_w <= tcol + float(WIN_BUF)) & (okc > 0.5)
    kw_rows = jnp.concatenate([kwb_ref[0, 0].T, kwn_ref[0], zpad], axis=0).astype(BF16)
    vw_rows = jnp.concatenate([vwb_ref[0, 0].T, vwn_ref[0], zpad], axis=0).astype(BF16)
    o_w = attend(kw_rows, vw_rows, mask_w, float(WIN_BUF) + tcol - i_w)
    gates = jax.nn.sigmoid(gate_ref[0])
    o_ref[0] = _own_group_lanes(gates[:, 0:1] * oc_ref[0] + gates[:, 1:2] * o_s + gates[:, 2:3] * o_w)


def sample_sel_dense(l, pool_k, pool_v, page_table, idx, knew, vnew, kwin, vwin, kwnew, vwnew, q32, gate32, oc):
    B = page_table.shape[0]
    info, _, _, _ = _col_consts()
    ntg = DEC_SEQ * KV_GROUPS
    per_page = PAGE_SIZE // SLC_BLOCK
    kd, vd = sample_gather(l, pool_k, pool_v, page_table, idx)
    per_b = lambda b, pt, ix: (b, 0, 0)
    const = lambda b, pt, ix: (0, 0)
    dense = pl.BlockSpec((1, ntg, DYN_ROWS, LANE), lambda b, pt, ix: (b, 0, 0, 0))
    first = pl.BlockSpec((1, 1, LANE, PAGE_SIZE), lambda b, pt, ix: (l, pt[b, 0], 0, 0))
    prev = pl.BlockSpec((1, 1, LANE, PAGE_SIZE), lambda b, pt, ix: (l, pt[b, N_PAGES - 1], 0, 0))
    new8 = pl.BlockSpec((1, 8, LANE), per_b)
    win = pl.BlockSpec((1, 1, LANE, WIN_BUF), lambda b, pt, ix: (l, b, 0, 0))
    col32 = pl.BlockSpec((1, NCOL, LANE), per_b)
    gs = pltpu.PrefetchScalarGridSpec(
        num_scalar_prefetch=2, grid=(B,),
        in_specs=[dense, dense, first, prev, first, prev, new8, new8, win, win, new8, new8, col32, col32, col32,
                  pl.BlockSpec((8, LANE), const), pl.BlockSpec((ROWS_ALL, LANE), const)],
        out_specs=col32,
        scratch_shapes=[pltpu.VMEM((ROWS_ALL, LANE), F32)])
    return pl.pallas_call(
        _sample_sel_kernel, grid_spec=gs,
        out_shape=jax.ShapeDtypeStruct((B, NCOL, LANE), F32),
        compiler_params=pltpu.CompilerParams(dimension_semantics=("arbitrary",), vmem_limit_bytes=VMEM_LIMIT),
        name="sample_sel",
    )(page_table, idx, kd, vd, pool_k, pool_k, pool_v, pool_v, knew, vnew, kwin, vwin, kwnew, vwnew,
      q32, gate32, oc, jnp.asarray(info), jnp.asarray(_sel_mask()))


def nsa_sample_attn(l, proj, pools, win, page_table, cw):
    B = page_table.shape[0]
    T = DEC_SEQ
    pool_kc, pool_vc, pool_ks, pool_vs = pools
    n_pool = pool_kc.shape[1]
    pages = lambda p: p.transpose(0, 1, 3, 4, 2).reshape(DEPTH, n_pool, LANE, PAGE_SIZE)
    kc_s, vc_st = compress_paged(l, pages(pool_kc), pages(pool_vc), page_table, cw)
    q32 = proj[:, COL_Q:COL_Q + N_HEADS * LANE].reshape(B, NCOL, LANE)
    oc, idx = sample_cmp(q32, kc_s, vc_st)
    idx = idx[:, :, :T * KV_GROUPS].reshape(B, N_SELECT * T * KV_GROUPS)

    def new8(c):
        return jnp.pad(proj[:, c:c + LANE].reshape(B, T, LANE), ((0, 0), (0, 8 - T), (0, 0)))
    small = proj[:, COL_SMALL:COL_SMALL + LANE].reshape(B, T, LANE)
    gate32 = jnp.pad(small[:, :, SMALL_G:SMALL_G + 3 * N_HEADS].reshape(B, NCOL, 3), ((0, 0), (0, 0), (0, LANE - 3)))
    wins = [w.transpose(0, 1, 3, 4, 2).reshape(DEPTH, B, LANE, WIN_BUF) for w in win]
    o32 = sample_sel_dense(l, pages(pool_ks), pages(pool_vs), page_table, idx, new8(COL_KV + 2 * LANE), new8(COL_KV + 3 * LANE),
                     wins[0], wins[1], new8(COL_KV + 4 * LANE), new8(COL_KV + 5 * LANE), q32, gate32, oc)
    o4 = o32.reshape(B, T, N_HEADS, KV_GROUPS, HEAD_DIM)
    o = jnp.concatenate([o4[:, :, :HEADS_PER_GROUP, 0], o4[:, :, HEADS_PER_GROUP:, 1]], axis=2)
    return o.reshape(B * T, D_B)


def _layer(x, B, T, lw, state, cmp, sample_ctx, g_final, final_norm):
    proj = rms_proj(x, lw["g_mix"], lw["w_in"], lw["b_in"])

    def piece(c0, width):
        return proj[:, c0:c0 + width].reshape(B, T, width)

    m_qk, m_v, m_o = piece(COL_QK, 2 * D_A), piece(COL_V, D_A), piece(COL_O, D_A)
    small = piece(COL_SMALL, LANE)
    kv = [piece(COL_KV + i * KV_W, KV_W).reshape(B, T, KV_GROUPS, HEAD_DIM) for i in range(6)]
    kc, vc, ks, vs, kw, vw = kv

    conv_buf, C, n, m = state
    conv_new = jnp.concatenate([conv_buf, m_qk], axis=1)[:, T:]
    margs = (conv_buf, C, n, m, lw["conv_w"], lw["conv_b"], lw["m_norm"])
    if sample_ctx is None:
        h_a, C1, n1, m1 = mlstm(B, T, proj, COL_QK // (2 * D_A), proj, COL_V // D_A, proj, COL_O // D_A, proj, COL_SMALL // LANE, *margs)
    else:
        def pad_tok(a, fill):
            return jnp.concatenate([a, jnp.broadcast_to(fill, (B, M_CHUNK - T, a.shape[-1]))], axis=1).reshape(B * M_CHUNK, a.shape[-1])
        fill_small = jnp.zeros((LANE,), F32).at[SMALL_I:SMALL_I + M_HEADS].set(NEG).at[SMALL_F:SMALL_F + M_HEADS].set(BIG)
        zero = jnp.zeros((1,), F32)
        h_pad, C1, n1, m1 = mlstm(B, M_CHUNK, pad_tok(m_qk, zero), 0, pad_tok(m_v, zero), 0, pad_tok(m_o, zero), 0, pad_tok(small, fill_small), 0, *margs)
        h_a = h_pad.reshape(B, M_CHUNK, D_A)[:, :T].reshape(B * T, D_A)
    cw = _cmp_weights(*cmp)
    if sample_ctx is None:
        kvc = COL_KV // LANE
        kc_s, vc_st = compress_prompt(B, T, proj, kvc, proj, kvc + 1, cw)
        o_b = nsa_prompt_attn(B, T, proj, kc_s, vc_st)
        keep = min(WINDOW, T)
        kw_state, vw_state = kw[:, T - keep:], vw[:, T - keep:]
    else:
        l, pools, win, page_table = sample_ctx
        o_b = nsa_sample_attn(l, proj, pools, win, page_table, cw)
        keep = min(WINDOW, WIN_BUF + T)
        kw_state = jnp.concatenate([win[0][l], kw], axis=1)[:, WIN_BUF + T - keep:]
        vw_state = jnp.concatenate([win[1][l], vw], axis=1)[:, WIN_BUF + T - keep:]

    x = merge(x, h_a.reshape(B * T, D_A), o_b.reshape(B * T, D_B), proj, lw["w_a"], lw["w_b"], lw["w_o"])
    x = ffn(x, lw["g_ffn"], lw["w_ffn_in"], lw["w_ffn_out"], g_final, final_norm)
    return x, (kc, vc, ks, vs, kw_state, vw_state, conv_new, C1, n1, m1)


def kernel(x_prompt, x_sample, cache_k_cmp, cache_v_cmp, cache_k_slc, cache_v_slc, state_k_win, state_v_win, state_conv, state_C, state_n, state_m, page_table, norm_mix, w_in, b_in, conv_w, conv_b, cmp_pos_k, cmp_pos_v, cmp_proj_k, cmp_proj_v, m_norm, w_a, w_b, w_o, norm_ffn, w_ffn_in, w_ffn_out, norm_final):
    yp = x_prompt.reshape(BATCH * SEQ, D_MODEL)
    ys = x_sample.reshape(DEC_BATCH * DEC_SEQ, D_MODEL)
    p_states, s_states = [], []
    for l in range(DEPTH):
        w_in_l, b_in_l = _prep_w_in(w_in[l], b_in[l])
        lw = dict(g_mix=norm_mix[l], w_in=w_in_l, b_in=b_in_l, conv_w=conv_w[l], conv_b=conv_b[l], m_norm=m_norm[l],
                  w_a=w_a[l].astype(BF16), w_b=w_b[l].astype(BF16), w_o=w_o[l].astype(BF16), g_ffn=norm_ffn[l],
                  w_ffn_in=w_ffn_in[l].astype(BF16), w_ffn_out=w_ffn_out[l].astype(BF16))
        cmp = (cmp_pos_k[l], cmp_pos_v[l], cmp_proj_k[l], cmp_proj_v[l])
        last = l == DEPTH - 1
        conv0 = jnp.zeros((BATCH, CONV_W - 1, 2 * D_A), F32)
        C0 = jnp.zeros((BATCH, M_HEADS, M_HEAD_DIM, M_HEAD_DIM), F32)
        n0 = jnp.zeros((BATCH, M_HEADS, M_HEAD_DIM), F32)
        m0 = jnp.full((BATCH, M_HEADS), NEG, F32)
        yp, st_p = _layer(yp, BATCH, SEQ, lw, (conv0, C0, n0, m0), cmp, None, norm_final, last)
        p_states.append(st_p)
        ctx = (l, (cache_k_cmp, cache_v_cmp, cache_k_slc, cache_v_slc), (state_k_win, state_v_win), page_table)
        ys, st_s = _layer(ys, DEC_BATCH, DEC_SEQ, lw, (state_conv[l], state_C[l], state_n[l], state_m[l]), cmp, ctx, norm_final, last)
        s_states.append(st_s)
    y_prompt = yp.reshape(BATCH, SEQ, D_MODEL)
    y_sample = ys.reshape(DEC_BATCH, DEC_SEQ, D_MODEL)
    P = [jnp.stack(a) for a in zip(*p_states)]
    S = [jnp.stack(a) for a in zip(*s_states)]
    return (y_prompt, y_sample, P[0], S[0], P[1], S[1], P[2], S[2], P[3], S[3], P[4], S[4],
            P[5], S[5], P[6], S[6], P[7], S[7], P[8], S[8], P[9], S[9])
```

```python
import functools
import jax, jax.numpy as jnp
from jax import lax
import numpy as np
from jax.experimental import pallas as pl
from jax.experimental.pallas import tpu as pltpu

D_MODEL = 1024
BATCH = 2
SEQ = 8192
DEPTH = 2
DEC_BATCH = 32
DEC_SEQ = 4
PAST_LEN = 16384
PAGE_SIZE = 128
M_HEADS = 4
M_HEAD_DIM = D_MODEL // 8
D_A = M_HEADS * M_HEAD_DIM
CONV_W = 4
M_CHUNK = 128
N_HEADS = 8
HEAD_DIM = D_MODEL // 16
D_B = N_HEADS * HEAD_DIM
KV_GROUPS = 2
HEADS_PER_GROUP = N_HEADS // KV_GROUPS
KV_W = KV_GROUPS * HEAD_DIM
CMP_STRIDE = 16
CMP_BLOCK = 2 * CMP_STRIDE
SLC_BLOCK = 64
N_SELECT = 16
WINDOW = 512
Q_BLOCK = 128
D_FF = -(-8 * D_MODEL // (3 * 256)) * 256
RMS_EPS = 1e-6
NEG = -1e30
BIG = 1e30
SIZES = (2 * D_A, D_A, D_A, M_HEADS, M_HEADS, D_B, KV_W, KV_W, KV_W, KV_W, KV_W, KV_W, 3 * N_HEADS, 2 * D_MODEL)
SPLITS = tuple(int(s) for s in np.cumsum(SIZES)[:-1])
D_IN = sum(SIZES)

F32 = jnp.float32
BF16 = jnp.bfloat16
LANE = 128
VMEM_LIMIT = 48 * 1024 * 1024

COL_QK = 0
COL_GMA = 1024
COL_GMB = 2048
COL_V = 3072
COL_O = 3584
COL_Q = 4096
COL_KV = 5120
COL_SMALL = 5888
N_PROJ = 6144
SMALL_I = 0
SMALL_F = M_HEADS
SMALL_G = 2 * M_HEADS


def _prep_w_in(w, b):
    def cols(a):
        (m_qk, m_v, m_o, m_i, m_f, a_q, a_kc, a_vc, a_ks, a_vs, a_kw, a_vw, a_g, g_mrg) = jnp.split(a, SPLITS, axis=-1)
        z64 = jnp.zeros(a.shape[:-1] + (HEAD_DIM,), a.dtype)
        slabs = []
        for h in range(N_HEADS):
            qh = a_q[..., h * HEAD_DIM:(h + 1) * HEAD_DIM]
            slabs += [qh, z64] if h < HEADS_PER_GROUP else [z64, qh]
        small = jnp.concatenate([m_i, m_f, a_g, jnp.zeros(a.shape[:-1] + (LANE - 2 * M_HEADS - 3 * N_HEADS,), a.dtype)], -1)
        pad = jnp.zeros(a.shape[:-1] + (N_PROJ - COL_SMALL - LANE,), a.dtype)
        return jnp.concatenate([m_qk, g_mrg, m_v, m_o] + slabs + [a_kc, a_vc, a_ks, a_vs, a_kw, a_vw, small, pad], -1)
    return cols(w).astype(BF16), cols(b[None, :])


def _rms_proj_kernel(x_ref, g_ref, w_ref, b_ref, o_ref, h_scr):
    @pl.when(pl.program_id(1) == 0)
    def _():
        x = x_ref[...]
        ms = jnp.mean(x * x, axis=-1, keepdims=True)
        h_scr[...] = (x * lax.rsqrt(ms + RMS_EPS) * g_ref[...]).astype(BF16)
    o_ref[...] = jnp.dot(h_scr[...], w_ref[...], preferred_element_type=F32) + b_ref[...]


def rms_proj(x, g, w, b):
    T, D = x.shape
    N = w.shape[1]
    tm = min(T, 1024)
    tn = 2048
    return pl.pallas_call(
        _rms_proj_kernel,
        grid=(T // tm, N // tn),
        in_specs=[pl.BlockSpec((tm, D), lambda i, j: (i, 0)),
                  pl.BlockSpec((1, D), lambda i, j: (0, 0)),
                  pl.BlockSpec((D, tn), lambda i, j: (0, j)),
                  pl.BlockSpec((1, tn), lambda i, j: (0, j))],
        out_specs=pl.BlockSpec((tm, tn), lambda i, j: (i, j)),
        out_shape=jax.ShapeDtypeStruct((T, N), F32),
        scratch_shapes=[pltpu.VMEM((tm, D), BF16)],
        compiler_params=pltpu.CompilerParams(dimension_semantics=("arbitrary", "arbitrary"), vmem_limit_bytes=VMEM_LIMIT),
        name="rms_proj",
    )(x, g.reshape(1, D), w, b)


def _merge_kernel(x_ref, ha_ref, ob_ref, ga_ref, gb_ref, wa_ref, wb_ref, wo_ref, o_ref):
    ya = jnp.dot(ha_ref[...].astype(BF16), wa_ref[...], preferred_element_type=F32)
    yb = jnp.dot(ob_ref[...].astype(BF16), wb_ref[...], preferred_element_type=F32)
    y = jax.nn.sigmoid(ga_ref[...]) * ya + jax.nn.sigmoid(gb_ref[...]) * yb
    o_ref[...] = x_ref[...] + jnp.dot(y.astype(BF16), wo_ref[...], preferred_element_type=F32)


def merge(x, ha, ob, proj, wa, wb, wo):
    T, D = x.shape
    tm = min(T, 512)
    row = lambda i: (i, 0)
    const = lambda i: (0, 0)
    return pl.pallas_call(
        _merge_kernel,
        grid=(T // tm,),
        in_specs=[pl.BlockSpec((tm, D), row),
                  pl.BlockSpec((tm, D_A), row),
                  pl.BlockSpec((tm, D_B), row),
                  pl.BlockSpec((tm, D), lambda i: (i, COL_GMA // D_MODEL)),
                  pl.BlockSpec((tm, D), lambda i: (i, COL_GMB // D_MODEL)),
                  pl.BlockSpec((D_A, D), const),
                  pl.BlockSpec((D_B, D), const),
                  pl.BlockSpec((D, D), const)],
        out_specs=pl.BlockSpec((tm, D), row),
        out_shape=jax.ShapeDtypeStruct((T, D), F32),
        compiler_params=pltpu.CompilerParams(dimension_semantics=("arbitrary",), vmem_limit_bytes=VMEM_LIMIT),
        name="merge",
    )(x, ha, ob, proj, proj, wa, wb, wo)


def _ffn_kernel(x_ref, g_ref, wg_ref, wu_ref, wo_ref, gf_ref, o_ref, h_scr, acc_scr, *, final_norm):
    j = pl.program_id(1)

    @pl.when(j == 0)
    def _():
        x = x_ref[...]
        ms = jnp.mean(x * x, axis=-1, keepdims=True)
        h_scr[...] = (x * lax.rsqrt(ms + RMS_EPS) * g_ref[...]).astype(BF16)
        acc_scr[...] = jnp.zeros_like(acc_scr)

    h = h_scr[...]
    gate = jnp.dot(h, wg_ref[...], preferred_element_type=F32)
    up = jnp.dot(h, wu_ref[...], preferred_element_type=F32)
    act = (gate * jax.nn.sigmoid(gate) * up).astype(BF16)
    acc_scr[...] += jnp.dot(act, wo_ref[...], preferred_element_type=F32)

    @pl.when(j == pl.num_programs(1) - 1)
    def _():
        y = x_ref[...] + acc_scr[...]
        if final_norm:
            ms = jnp.mean(y * y, axis=-1, keepdims=True)
            y = y * lax.rsqrt(ms + RMS_EPS) * gf_ref[...]
        o_ref[...] = y


def ffn(x, g, w_in, w_out, g_final, final_norm):
    T, D = x.shape
    tm = min(T, 512)
    tf = D_FF // 2
    nf = D_FF // tf
    return pl.pallas_call(
        functools.partial(_ffn_kernel, final_norm=final_norm),
        grid=(T // tm, nf),
        in_specs=[pl.BlockSpec((tm, D), lambda i, j: (i, 0)),
                  pl.BlockSpec((1, D), lambda i, j: (0, 0)),
                  pl.BlockSpec((D, tf), lambda i, j: (0, j)),
                  pl.BlockSpec((D, tf), lambda i, j: (0, j + nf)),
                  pl.BlockSpec((tf, D), lambda i, j: (j, 0)),
                  pl.BlockSpec((1, D), lambda i, j: (0, 0))],
        out_specs=pl.BlockSpec((tm, D), lambda i, j: (i, 0)),
        out_shape=jax.ShapeDtypeStruct((T, D), F32),
        scratch_shapes=[pltpu.VMEM((tm, D), BF16), pltpu.VMEM((tm, D), F32)],
        compiler_params=pltpu.CompilerParams(dimension_semantics=("arbitrary", "arbitrary"), vmem_limit_bytes=VMEM_LIMIT),
        name="ffn",
    )(x, g.reshape(1, D), w_in, w_in, w_out, g_final.reshape(1, D))


HI = lax.Precision.HIGHEST


def _mm(a, b):
    return jnp.dot(a, b, preferred_element_type=F32)


def _log_sigmoid(x):
    return jnp.minimum(x, 0.0) - jnp.log1p(jnp.exp(-jnp.abs(x)))


def _mlstm_kernel(qk_ref, v_ref, o_ref, small_ref, cbuf_ref, c0_ref, n0_ref, m0_ref, cw_ref, cb_ref, g_ref,
                  h_ref, c_out_ref, n_out_ref, m_out_ref, ext_scr, c_scr, n_scr, m_scr):
    L = M_CHUNK
    c = pl.program_id(1)

    @pl.when(c == 0)
    def _():
        ext_scr[0:8, :] = cbuf_ref[0]
        c_scr[...] = c0_ref[0]
        n_scr[...] = n0_ref[0]
        m_scr[...] = m0_ref[0]

    ext_scr[8:8 + L, :] = qk_ref[...]
    conv = cb_ref[...] + ext_scr[5:5 + L, :] * cw_ref[0:1, :]
    for j in range(1, CONV_W):
        conv = conv + ext_scr[5 + j:5 + j + L, :] * cw_ref[j:j + 1, :]
    ext_scr[0:8, :] = qk_ref[L - 8:L, :]
    qk = conv * jax.nn.sigmoid(conv)

    small = small_ref[...]
    small_t = small.T
    row = lax.broadcasted_iota(jnp.int32, (L, L), 0)
    col = lax.broadcasted_iota(jnp.int32, (L, L), 1)
    causal = col <= row
    fcol_all = jnp.dot(causal.astype(F32), _log_sigmoid(small), preferred_element_type=F32, precision=HI)
    frow_all = jnp.dot(_log_sigmoid(small_t), (row <= col).astype(F32), preferred_element_type=F32, precision=HI)

    for h in range(M_HEADS):
        sl = slice(h * M_HEAD_DIM, (h + 1) * M_HEAD_DIM)
        q = qk[:, sl]
        k = qk[:, D_A + h * M_HEAD_DIM:D_A + (h + 1) * M_HEAD_DIM] * (M_HEAD_DIM ** -0.5)
        fc = fcol_all[:, SMALL_F + h:SMALL_F + h + 1]
        fr = frow_all[SMALL_F + h:SMALL_F + h + 1, :]
        igr = small_t[SMALL_I + h:SMALL_I + h + 1, :]
        igc = small[:, SMALL_I + h:SMALL_I + h + 1]
        m_prev = m_scr[h:h + 1, 0:1]
        d = jnp.where(causal, fc - fr + igr, NEG)
        a = fc + m_prev
        m_t = jnp.maximum(a, jnp.max(d, axis=-1, keepdims=True))
        qb = q.astype(BF16)
        kb = k.astype(BF16)
        vb = v_ref[:, sl].astype(BF16)
        s = lax.dot_general(qb, kb, (((1,), (1,)), ((), ())), preferred_element_type=F32) * jnp.exp(d - m_t)
        inter = jnp.exp(a - m_t)
        num = _mm(s.astype(BF16), vb) + inter * _mm(qb, c_scr[h].astype(BF16))
        den = jnp.sum(s, axis=-1, keepdims=True) + inter * jnp.sum(q * n_scr[h:h + 1, :], axis=-1, keepdims=True)
        hh = num / jnp.maximum(jnp.abs(den), jnp.exp(-m_t))
        mu = jnp.mean(hh, axis=-1, keepdims=True)
        var = jnp.mean(jnp.square(hh - mu), axis=-1, keepdims=True)
        hn = (hh - mu) * lax.rsqrt(var + RMS_EPS) * g_ref[:, sl]
        h_ref[:, sl] = jax.nn.sigmoid(o_ref[:, sl]) * hn
        f_l = fr[:, L - 1:L]
        m_new = jnp.maximum(f_l + m_prev, jnp.max(f_l - fr + igr, axis=-1, keepdims=True))
        decay = jnp.exp(f_l + m_prev - m_new)
        ks = k * jnp.exp(f_l - fc + igc - m_new)
        c_scr[h] = decay * c_scr[h] + lax.dot_general(ks.astype(BF16), vb, (((0,), (0,)), ((), ())), preferred_element_type=F32)
        n_scr[h:h + 1, :] = decay * n_scr[h:h + 1, :] + jnp.sum(ks, axis=0, keepdims=True)
        m_scr[h:h + 1, :] = jnp.broadcast_to(m_new, (1, LANE))

    @pl.when(c == pl.num_programs(1) - 1)
    def _():
        c_out_ref[0] = c_scr[...]
        n_out_ref[0] = n_scr[...]
        m_out_ref[0] = m_scr[...]


def mlstm(B, T, qk, qk_col, v, v_col, o, o_col, small, small_col, conv_buf, C0, n0, m0, conv_w, conv_b, head_g):
    L = M_CHUNK
    nch = T // L
    cbuf = jnp.concatenate([jnp.zeros((B, 8 - (CONV_W - 1), 2 * D_A), F32), conv_buf], axis=1)
    n0p = jnp.concatenate([n0, jnp.zeros((B, 8 - M_HEADS, M_HEAD_DIM), F32)], axis=1)
    m0p = jnp.broadcast_to(jnp.concatenate([m0, jnp.zeros((B, 8 - M_HEADS), F32)], axis=1)[:, :, None], (B, 8, LANE))
    tok = lambda cb: (lambda b, c: (b * nch + c, cb))
    per_b3 = lambda b, c: (b, 0, 0)
    per_b4 = lambda b, c: (b, 0, 0, 0)
    const = lambda b, c: (0, 0)
    h, C1, n1, m1 = pl.pallas_call(
        _mlstm_kernel,
        grid=(B, nch),
        in_specs=[pl.BlockSpec((L, 2 * D_A), tok(qk_col)),
                  pl.BlockSpec((L, D_A), tok(v_col)),
                  pl.BlockSpec((L, D_A), tok(o_col)),
                  pl.BlockSpec((L, LANE), tok(small_col)),
                  pl.BlockSpec((1, 8, 2 * D_A), per_b3),
                  pl.BlockSpec((1, M_HEADS, M_HEAD_DIM, M_HEAD_DIM), per_b4),
                  pl.BlockSpec((1, 8, M_HEAD_DIM), per_b3),
                  pl.BlockSpec((1, 8, LANE), per_b3),
                  pl.BlockSpec((CONV_W, 2 * D_A), const),
                  pl.BlockSpec((1, 2 * D_A), const),
                  pl.BlockSpec((1, D_A), const)],
        out_specs=[pl.BlockSpec((L, D_A), tok(0)),
                   pl.BlockSpec((1, M_HEADS, M_HEAD_DIM, M_HEAD_DIM), per_b4),
                   pl.BlockSpec((1, 8, M_HEAD_DIM), per_b3),
                   pl.BlockSpec((1, 8, LANE), per_b3)],
        out_shape=[jax.ShapeDtypeStruct((B * T, D_A), F32),
                   jax.ShapeDtypeStruct((B, M_HEADS, M_HEAD_DIM, M_HEAD_DIM), F32),
                   jax.ShapeDtypeStruct((B, 8, M_HEAD_DIM), F32),
                   jax.ShapeDtypeStruct((B, 8, LANE), F32)],
        scratch_shapes=[pltpu.VMEM((8 + L, 2 * D_A), F32),
                        pltpu.VMEM((M_HEADS, M_HEAD_DIM, M_HEAD_DIM), F32),
                        pltpu.VMEM((8, M_HEAD_DIM), F32),
                        pltpu.VMEM((8, LANE), F32)],
        compiler_params=pltpu.CompilerParams(dimension_semantics=("arbitrary", "arbitrary"), vmem_limit_bytes=VMEM_LIMIT),
        name="mlstm",
    )(qk, v, o, small, cbuf, C0, n0p, m0p, conv_w, conv_b.reshape(1, -1), head_g.reshape(1, -1))
    return h, C1, n1[:, :M_HEADS], m1[:, :M_HEADS, 0]


SLOPES = tuple(float(2.0 ** (-8.0 * (h + 1) / N_HEADS)) for h in range(N_HEADS))
Q_SCALE = HEAD_DIM ** -0.5
REMOVED = -3.0e38
SEL_CHAIN = 6


def _pool(x, w_ref, n):
    x3 = x.reshape(n, CMP_STRIDE, LANE)
    head = jnp.sum(x3 * w_ref[0:CMP_STRIDE, :][None], axis=1)
    tail = jnp.sum(x3 * w_ref[CMP_STRIDE:CMP_BLOCK, :][None], axis=1)
    nxt = pltpu.roll(tail, n - 1, 0)
    last = lax.broadcasted_iota(jnp.int32, (n, LANE), 0) == n - 1
    return head + jnp.where(last, 0.0, nxt)


def _compress_kernel(k_ref, v_ref, wk_ref, wv_ref, pk_ref, pv_ref, kc_ref, vct_ref, *, n):
    pk = _pool(k_ref[...], wk_ref, n)
    pv = _pool(v_ref[...], wv_ref, n)
    kc_ref[0] = _mm(pk.astype(BF16), pk_ref[...]).astype(BF16)
    vct_ref[0] = _mm(pv.astype(BF16), pv_ref[...]).T.astype(BF16)


def _cmp_weights(pos_k, pos_v, proj_k, proj_v):
    tile = lambda w: jnp.concatenate([w] * KV_GROUPS, axis=1)
    bd = lambda w: jnp.kron(jnp.eye(KV_GROUPS, dtype=F32), w).astype(BF16)
    return tile(pos_k), tile(pos_v), bd(proj_k), bd(proj_v)


def compress_prompt(B, S, kc, kc_col, vc, vc_col, cw):
    n = S // CMP_STRIDE
    wk, wv, pk, pv = cw
    const = lambda b: (0, 0)
    return pl.pallas_call(
        functools.partial(_compress_kernel, n=n),
        grid=(B,),
        in_specs=[pl.BlockSpec((S, LANE), lambda b: (b, kc_col)),
                  pl.BlockSpec((S, LANE), lambda b: (b, vc_col)),
                  pl.BlockSpec((CMP_BLOCK, LANE), const), pl.BlockSpec((CMP_BLOCK, LANE), const),
                  pl.BlockSpec((LANE, LANE), const), pl.BlockSpec((LANE, LANE), const)],
        out_specs=[pl.BlockSpec((1, n, LANE), lambda b: (b, 0, 0)), pl.BlockSpec((1, LANE, n), lambda b: (b, 0, 0))],
        out_shape=[jax.ShapeDtypeStruct((B, n, LANE), BF16), jax.ShapeDtypeStruct((B, LANE, n), BF16)],
        compiler_params=pltpu.CompilerParams(dimension_semantics=("arbitrary",), vmem_limit_bytes=VMEM_LIMIT),
        name="compress_prompt",
    )(kc, vc, wk, wv, pk, pv)


def _nsa_prompt_kernel(q_ref, small_ref, kc_ref, vct_ref, ks_ref, vs_ref, kw_ref, vw_ref, o_ref,
                       ksb, vstb, kwb, vwtb, qt_scr, sd_scr, p_scr, sel_scr, m_scr, l_scr, acc_scr, oc_scr, *, S):
    QB = Q_BLOCK
    nkb = S // QB
    assert nkb <= 64
    ncb = S // CMP_STRIDE
    nsb = S // SLC_BLOCK
    G, I, HD = KV_GROUPS, HEADS_PER_GROUP, HEAD_DIM
    qb = pl.program_id(1)
    c_i = lax.broadcasted_iota(jnp.int32, (QB, QB), 0)
    r_i = lax.broadcasted_iota(jnp.int32, (QB, QB), 1)

    @pl.when(qb == 0)
    def _():
        def cp(i, carry):
            rows = pl.ds(pl.multiple_of(i * QB, QB), QB)
            ksb[i] = ks_ref[rows, :].astype(BF16)
            kwb[i] = kw_ref[rows, :].astype(BF16)
            vstb[i] = vs_ref[rows, :].T.astype(BF16)
            vwtb[i] = vw_ref[rows, :].T.astype(BF16)
            return carry
        lax.fori_loop(0, nkb, cp, 0)
        rc = (r_i - c_i).astype(F32)
        for h in range(N_HEADS):
            sd_scr[h] = SLOPES[h] * rc
        p_scr[0:8, :] = jnp.zeros((8, LANE), F32)
        p_scr[8 + ncb:16 + ncb, :] = jnp.zeros((8, LANE), F32)

    for h in range(N_HEADS):
        g, i = divmod(h, I)
        qt_scr[g, :, i * QB:(i + 1) * QB] = (q_ref[:, h * LANE:(h + 1) * LANE] * Q_SCALE).T.astype(BF16)
    small_t = small_ref[...].T
    gates_t = jax.nn.sigmoid(small_t[SMALL_G:SMALL_G + 3 * N_HEADS, :])
    m_scr[...] = jnp.full(m_scr.shape, NEG, F32)
    l_scr[...] = jnp.zeros(l_scr.shape, F32)
    acc_scr[...] = jnp.zeros(acc_scr.shape, F32)

    n_i = lax.broadcasted_iota(jnp.int32, (ncb, QB), 0)
    rr_i = lax.broadcasted_iota(jnp.int32, (ncb, QB), 1)
    dist_c = (rr_i - CMP_STRIDE * n_i - (CMP_BLOCK - 1) + QB * qb).astype(F32)
    mask_c = dist_c >= 0.0
    j_i = lax.broadcasted_iota(jnp.int32, (nsb, QB), 0)
    tq = lax.broadcasted_iota(jnp.int32, (nsb, QB), 1) + QB * qb
    cur = lax.shift_right_arithmetic(tq, 6)
    forced = (j_i == 0) | (j_i == cur) | (j_i == cur - 1)
    valid = j_i * SLC_BLOCK <= tq
    jf = j_i.astype(F32)
    jb_i = lax.broadcasted_iota(jnp.int32, (nsb, 1), 0)
    need_words = []
    for g in range(G):
        s4 = _mm(kc_ref[0], qt_scr[g])
        psum = jnp.zeros((ncb, QB), F32)
        pns = []
        for i in range(I):
            h = g * I + i
            s = jnp.where(mask_c, s4[:, i * QB:(i + 1) * QB] - SLOPES[h] * dist_c, NEG)
            e = jnp.exp(s - jnp.max(s, axis=0, keepdims=True))
            pn = jnp.where(mask_c, e, 0.0) / jnp.sum(e, axis=0, keepdims=True)
            psum = psum + pn
            pns.append(pn.astype(BF16))
        oc_scr[g] = _mm(vct_ref[0][g * HD:(g + 1) * HD, :], jnp.concatenate(pns, axis=1))
        p_scr[8:8 + ncb, :] = psum
        imp = p_scr[pl.ds(7, nsb, stride=4), :]
        for d in range(1, 5):
            imp = imp + p_scr[pl.ds(7 + d, nsb, stride=4), :]
        score = jnp.where(forced, BIG, jnp.where(valid, imp, NEG))
        sel = jnp.zeros((nsb, QB), F32)
        for _ in range(min(N_SELECT, nsb)):
            cm = jnp.max(score, axis=0, keepdims=True)
            idx = jnp.min(jnp.where(score == cm, jf, 1e9), axis=0, keepdims=True)
            hit = jf == idx
            sel = jnp.where(hit, 1.0, sel)
            score = jnp.where(hit, REMOVED, score)
        sel_scr[g] = sel
        used = jnp.max(sel, axis=1, keepdims=True) > 0.5
        kb_j = lax.shift_right_logical(jb_i, 1)
        bits = jnp.where(used, lax.shift_left(jnp.ones_like(jb_i), kb_j & 31), 0)

        def word(half, parity):
            return jnp.sum(jnp.where((lax.shift_right_logical(kb_j, 5) == half) & ((jb_i & 1) == parity), bits, 0))
        need_words.append((word(0, 0) | word(0, 1), word(1, 0) | word(1, 1)))

    def sel_chain(kbs, g, diag_last):
        krows = jnp.concatenate([ksb[k] for k in kbs], axis=0)
        vcols = jnp.concatenate([vstb[k, g * HD:(g + 1) * HD, :] for k in kbs], axis=1)
        masks, offs = [], []
        for j, k in enumerate(kbs):
            m0 = sel_scr[g, pl.ds(2 * k, 1), :]
            m1 = sel_scr[g, pl.ds(2 * k + 1, 1), :]
            mk = jnp.concatenate([jnp.broadcast_to(m0, (SLC_BLOCK, QB)), jnp.broadcast_to(m1, (SLC_BLOCK, QB))], axis=0) > 0.5
            if diag_last and j == len(kbs) - 1:
                mk = mk & (c_i <= r_i)
            masks.append(mk)
            offs.append((qb - k).astype(F32) * float(QB))
        m_old = m_scr[g, 0:1, :]
        l_old = l_scr[g, 0:1, :]
        s4 = _mm(krows, qt_scr[g])
        ps, m_new, l_new, alphas = [], [], [], []
        for i in range(I):
            h = g * I + i
            cols = slice(i * QB, (i + 1) * QB)
            s = jnp.concatenate([jnp.where(masks[j], s4[j * QB:(j + 1) * QB, cols] - sd_scr[h] - SLOPES[h] * offs[j], NEG)
                                 for j in range(len(kbs))], axis=0)
            mo = m_old[:, cols]
            mn = jnp.maximum(mo, jnp.max(s, axis=0, keepdims=True))
            p = jnp.exp(s - mn)
            al = jnp.exp(mo - mn)
            ps.append(p.astype(BF16))
            m_new.append(mn)
            alphas.append(al)
            l_new.append(al * l_old[:, cols] + jnp.sum(p, axis=0, keepdims=True))
        acc_scr[g] = jnp.concatenate(alphas, axis=1) * acc_scr[g] + _mm(vcols, jnp.concatenate(ps, axis=1))
        m_scr[g, 0:1, :] = jnp.concatenate(m_new, axis=1)
        l_scr[g, 0:1, :] = jnp.concatenate(l_new, axis=1)

    none = jnp.int32(-1)
    for g in range(G):
        lo, hi = need_words[g]

        def sel_body(kb, pend, g=g, lo=lo, hi=hi):
            bit = (lax.shift_right_logical(jnp.where(kb < 32, lo, hi), kb & 31) & 1) != 0
            full = pend[-1] >= 0
            pl.when(jnp.logical_and(bit, full))(lambda: sel_chain(list(pend) + [kb], g, False))
            new = []
            for j, pj in enumerate(pend):
                first_free = (pj < 0) if j == 0 else jnp.logical_and(pend[j - 1] >= 0, pj < 0)
                new.append(jnp.where(bit, jnp.where(full, none, jnp.where(first_free, kb, pj)), pj))
            return tuple(new)
        pend = lax.fori_loop(0, qb, sel_body, (none,) * (SEL_CHAIN - 1))
        for n in range(SEL_CHAIN):
            has_n = (pend[0] < 0) if n == 0 else (pend[n - 1] >= 0)
            if n < SEL_CHAIN - 1:
                has_n = jnp.logical_and(has_n, pend[n] < 0) if n > 0 else has_n
            pl.when(has_n)(lambda n=n, pend=pend, g=g: sel_chain(list(pend[:n]) + [qb], g, True))

    nwb = WINDOW // QB
    kbs = [jnp.maximum(qb - nwb + d, 0) for d in range(nwb + 1)]
    kw_rows = jnp.concatenate([kwb[k] for k in kbs], axis=0)
    vw_cols = jnp.concatenate([vwtb[k] for k in kbs], axis=1)
    win_masks = []
    for d in range(nwb + 1):
        inside = (jnp.zeros_like(c_i) + (qb - nwb + d)) >= 0
        win_masks.append(inside & (r_i <= c_i) if d == 0 else ((c_i <= r_i) if d == nwb else inside))
    for g in range(G):
        s4 = _mm(kw_rows, qt_scr[g])
        ps, ls = [], []
        for i in range(I):
            h = g * I + i
            s = jnp.concatenate([jnp.where(win_masks[d], s4[d * QB:(d + 1) * QB, i * QB:(i + 1) * QB]
                                           - (sd_scr[h] + SLOPES[h] * float((nwb - d) * QB)), NEG)
                                 for d in range(nwb + 1)], axis=0)
            p = jnp.exp(s - jnp.max(s, axis=0, keepdims=True))
            ls.append(jnp.sum(p, axis=0, keepdims=True))
            ps.append(p.astype(BF16))
        acc_scr[G + g] = _mm(vw_cols[g * HD:(g + 1) * HD, :], jnp.concatenate(ps, axis=1))
        l_scr[G + g, 0:1, :] = jnp.concatenate(ls, axis=1)

    for pair in range(N_HEADS // 2):
        halves = []
        for h in (2 * pair, 2 * pair + 1):
            g, i = divmod(h, I)
            cols = slice(i * QB, (i + 1) * QB)
            o_s = acc_scr[g, :, cols] / l_scr[g, 0:1, cols]
            o_w = acc_scr[G + g, :, cols] / l_scr[G + g, 0:1, cols]
            o_c = oc_scr[g, :, cols]
            halves.append(gates_t[3 * h:3 * h + 1, :] * o_c + gates_t[3 * h + 1:3 * h + 2, :] * o_s + gates_t[3 * h + 2:3 * h + 3, :] * o_w)
        o_ref[:, pair * LANE:(pair + 1) * LANE] = jnp.concatenate(halves, axis=0).T


def nsa_prompt_attn(B, S, proj, kc_s, vc_st):
    nq = S // Q_BLOCK
    ncb = S // CMP_STRIDE
    nsb = S // SLC_BLOCK
    tok = lambda cb: (lambda b, q: (b * nq + q, cb))
    seq = lambda cb: (lambda b, q: (b, cb))
    one = pl.Buffered(1)
    kvc = COL_KV // LANE
    w4 = HEADS_PER_GROUP * Q_BLOCK
    return pl.pallas_call(
        functools.partial(_nsa_prompt_kernel, S=S),
        grid=(B, nq),
        in_specs=[pl.BlockSpec((Q_BLOCK, N_HEADS * LANE), tok(COL_Q // (N_HEADS * LANE))),
                  pl.BlockSpec((Q_BLOCK, LANE), tok(COL_SMALL // LANE)),
                  pl.BlockSpec((1, ncb, LANE), lambda b, q: (b, 0, 0)),
                  pl.BlockSpec((1, LANE, ncb), lambda b, q: (b, 0, 0)),
                  pl.BlockSpec((S, LANE), seq(kvc + 2), pipeline_mode=one),
                  pl.BlockSpec((S, LANE), seq(kvc + 3), pipeline_mode=one),
                  pl.BlockSpec((S, LANE), seq(kvc + 4), pipeline_mode=one),
                  pl.BlockSpec((S, LANE), seq(kvc + 5), pipeline_mode=one)],
        out_specs=pl.BlockSpec((Q_BLOCK, D_B), tok(0)),
        out_shape=jax.ShapeDtypeStruct((B * S, D_B), F32),
        scratch_shapes=[pltpu.VMEM((nq, Q_BLOCK, LANE), BF16), pltpu.VMEM((nq, LANE, Q_BLOCK), BF16),
                        pltpu.VMEM((nq, Q_BLOCK, LANE), BF16), pltpu.VMEM((nq, LANE, Q_BLOCK), BF16),
                        pltpu.VMEM((KV_GROUPS, LANE, w4), BF16),
                        pltpu.VMEM((N_HEADS, Q_BLOCK, Q_BLOCK), F32),
                        pltpu.VMEM((ncb + 16, Q_BLOCK), F32),
                        pltpu.VMEM((KV_GROUPS, nsb, Q_BLOCK), F32),
                        pltpu.VMEM((2 * KV_GROUPS, 8, w4), F32),
                        pltpu.VMEM((2 * KV_GROUPS, 8, w4), F32),
                        pltpu.VMEM((2 * KV_GROUPS, HEAD_DIM, w4), F32),
                        pltpu.VMEM((KV_GROUPS, HEAD_DIM, w4), F32)],
        compiler_params=pltpu.CompilerParams(dimension_semantics=("arbitrary", "arbitrary"), vmem_limit_bytes=VMEM_LIMIT),
        name="nsa_prompt",
    )(proj, proj, kc_s, vc_st, proj, proj, proj, proj)


N_PAGES = PAST_LEN // PAGE_SIZE
S_NCB = PAST_LEN // CMP_STRIDE
S_NSB = (PAST_LEN + SLC_BLOCK) // SLC_BLOCK
S_NSB_PAD = -(-S_NSB // 8) * 8
PAGES_PER_STEP = 16
CMP_STEPS = N_PAGES // PAGES_PER_STEP
NCOL = DEC_SEQ * N_HEADS
N_DYN = N_SELECT - 3
DYN_ROWS = N_DYN * SLC_BLOCK
ROW_DYN0 = 2 * SLC_BLOCK
ROW_NEW = ROW_DYN0 + DEC_SEQ * KV_GROUPS * DYN_ROWS
NEW_PAD = LANE
ROWS_ALL = ROW_NEW + NEW_PAD
WIN_BUF = min(WINDOW, PAST_LEN)
WIN_ROWS = WIN_BUF + NEW_PAD
assert ROWS_ALL % LANE == 0 and WIN_ROWS % LANE == 0
assert (PAST_LEN + DEC_SEQ - 1) // SLC_BLOCK == S_NSB - 1 and PAST_LEN % SLC_BLOCK == 0 and DEC_SEQ <= 8
assert CMP_STRIDE * S_NCB + CMP_BLOCK - 1 > PAST_LEN + DEC_SEQ - 1


def _col_consts():
    c = np.arange(LANE)
    t = c // N_HEADS
    h = c % N_HEADS
    ok = c < NCOL
    info = np.zeros((8, LANE), np.float32)
    info[0] = np.where(ok, 2.0 ** (-8.0 * (h + 1) / N_HEADS), 0.0)
    info[1] = np.where(ok, PAST_LEN + t, 0)
    info[2] = t
    info[3] = ok
    return info, t, h // HEADS_PER_GROUP, ok


def _sel_mask():
    _, t, g, ok = _col_consts()
    m = np.zeros((ROWS_ALL, LANE), np.float32)
    m[0:ROW_DYN0] = ok[None, :]
    for tt in range(DEC_SEQ):
        for gg in range(KV_GROUPS):
            r0 = ROW_DYN0 + (tt * KV_GROUPS + gg) * DYN_ROWS
            m[r0:r0 + DYN_ROWS] = (ok & (t == tt) & (g == gg))[None, :]
    r = np.arange(NEW_PAD)
    m[ROW_NEW:ROW_NEW + NEW_PAD] = (ok[None, :] & (r[:, None] <= t[None, :]) & (r[:, None] < DEC_SEQ))
    return m


def _compress_paged_kernel(pt_ref, *refs):
    P = PAGES_PER_STEP
    k_pages, k_next = refs[0:P], refs[P]
    v_pages, v_next = refs[P + 1:2 * P + 1], refs[2 * P + 1]
    wk_ref, wv_ref, pk_ref, pv_ref, kc_ref, vct_ref = refs[2 * P + 2:]
    n = P * PAGE_SIZE // CMP_STRIDE
    not_last = (pl.program_id(1) < pl.num_programs(1) - 1).astype(F32)
    last_row = lax.broadcasted_iota(jnp.int32, (n, LANE), 0) == n - 1

    def pooled(pages, nxt, w_ref):
        x = jnp.concatenate([p[0, 0].T for p in pages], axis=0)
        tail_next = jnp.sum(nxt[0, 0].T[0:CMP_STRIDE, :] * w_ref[CMP_STRIDE:CMP_BLOCK, :], axis=0, keepdims=True) * not_last
        return _pool(x, w_ref, n) + jnp.where(last_row, tail_next, 0.0)

    kc_ref[0] = _mm(pooled(k_pages, k_next, wk_ref).astype(BF16), pk_ref[...]).astype(BF16)
    vct_ref[0] = _mm(pooled(v_pages, v_next, wv_ref).astype(BF16), pv_ref[...]).T.astype(BF16)


def compress_paged(l, pool_k, pool_v, page_table, cw):
    B = page_table.shape[0]
    P = PAGES_PER_STEP
    wk, wv, pk, pv = cw
    n = P * PAGE_SIZE // CMP_STRIDE
    page = lambda i: pl.BlockSpec((1, 1, PAGE_SIZE, LANE), lambda b, j, pt: (l, pt[b, j * P + i], 0, 0))
    nxt = pl.BlockSpec((1, 1, PAGE_SIZE, LANE), lambda b, j, pt: (l, pt[b, jnp.minimum(j * P + P, N_PAGES - 1)], 0, 0))
    const = lambda b, j, pt: (0, 0)
    wspecs = [pl.BlockSpec((CMP_BLOCK, LANE), const), pl.BlockSpec((CMP_BLOCK, LANE), const),
              pl.BlockSpec((LANE, LANE), const), pl.BlockSpec((LANE, LANE), const)]
    gs = pltpu.PrefetchScalarGridSpec(
        num_scalar_prefetch=1, grid=(B, CMP_STEPS),
        in_specs=[page(i) for i in range(P)] + [nxt] + [page(i) for i in range(P)] + [nxt] + wspecs,
        out_specs=[pl.BlockSpec((1, n, LANE), lambda b, j, pt: (b, j, 0)), pl.BlockSpec((1, LANE, n), lambda b, j, pt: (b, 0, j))])
    return pl.pallas_call(
        _compress_paged_kernel, grid_spec=gs,
        out_shape=[jax.ShapeDtypeStruct((B, S_NCB, LANE), BF16), jax.ShapeDtypeStruct((B, LANE, S_NCB), BF16)],
        compiler_params=pltpu.CompilerParams(dimension_semantics=("arbitrary", "arbitrary"), vmem_limit_bytes=VMEM_LIMIT),
        name="compress_paged",
    )(page_table, *([pool_k] * (P + 1)), *([pool_v] * (P + 1)), wk, wv, pk, pv)


def _query_cols(q32):
    qp = jnp.concatenate([q32, jnp.zeros((LANE - NCOL, LANE), F32)], axis=0)
    return (qp * Q_SCALE).T.astype(BF16)


def _own_group_lanes(o):
    lane_g = lax.broadcasted_iota(jnp.int32, (NCOL, LANE), 1) // HEAD_DIM
    row_g = (lax.broadcasted_iota(jnp.int32, (NCOL, LANE), 0) % N_HEADS) // HEADS_PER_GROUP
    return jnp.where(lane_g == row_g, o, 0.0)


def _sample_cmp_kernel(q_ref, kc_ref, vct_ref, info_ref, grp_ref, oc_ref, idx_ref, p_scr):
    w = _query_cols(q_ref[0])
    slope = info_ref[0:1, :]
    qpos = info_ref[1:2, :]
    n_f = lax.broadcasted_iota(jnp.int32, (S_NCB, LANE), 0).astype(F32)
    dist = qpos - (CMP_STRIDE * n_f + (CMP_BLOCK - 1))
    mask = dist >= 0.0
    s = jnp.where(mask, _mm(kc_ref[0], w) - slope * dist, NEG)
    e = jnp.exp(s - jnp.max(s, axis=0, keepdims=True))
    pn = jnp.where(mask, e, 0.0) / jnp.sum(e, axis=0, keepdims=True)
    oc_ref[0] = _own_group_lanes(_mm(vct_ref[0], pn.astype(BF16)).T[0:NCOL, :])
    pg = jnp.dot(pn, grp_ref[...], preferred_element_type=F32, precision=HI)
    p_scr[0:8, :] = jnp.zeros((8, LANE), F32)
    p_scr[8:8 + S_NCB, :] = pg
    p_scr[8 + S_NCB:, :] = jnp.zeros((p_scr.shape[0] - 8 - S_NCB, LANE), F32)
    imp = p_scr[pl.ds(7, S_NSB_PAD, stride=4), :]
    for d in range(1, 5):
        imp = imp + p_scr[pl.ds(7 + d, S_NSB_PAD, stride=4), :]
    j_i = lax.broadcasted_iota(jnp.int32, (S_NSB_PAD, LANE), 0)
    tq = PAST_LEN + lax.broadcasted_iota(jnp.int32, (S_NSB_PAD, LANE), 1) // KV_GROUPS
    cur = lax.shift_right_arithmetic(tq, 6)
    forced = (j_i == 0) | (j_i == cur) | (j_i == cur - 1)
    valid = j_i * SLC_BLOCK <= tq
    score = jnp.where(j_i >= S_NSB, REMOVED, jnp.where(forced, BIG, jnp.where(valid, imp, NEG)))
    jf = j_i.astype(F32)
    picks = []
    for _ in range(N_SELECT):
        cm = jnp.max(score, axis=0, keepdims=True)
        idx = jnp.min(jnp.where(score == cm, jf, 1e9), axis=0, keepdims=True)
        score = jnp.where(jf == idx, REMOVED, score)
        picks.append(idx)
    idx_ref[0] = jnp.concatenate(picks, axis=0).astype(jnp.int32)


def sample_cmp(q32, kc_s, vc_st):
    B = q32.shape[0]
    info, t, g, ok = _col_consts()
    grp = np.zeros((LANE, LANE), np.float32)
    for c in range(NCOL):
        grp[c, t[c] * KV_GROUPS + g[c]] = 1.0
    per_b = lambda b: (b, 0, 0)
    const = lambda b: (0, 0)
    return pl.pallas_call(
        _sample_cmp_kernel, grid=(B,),
        in_specs=[pl.BlockSpec((1, NCOL, LANE), per_b), pl.BlockSpec((1, S_NCB, LANE), per_b), pl.BlockSpec((1, LANE, S_NCB), per_b),
                  pl.BlockSpec((8, LANE), const), pl.BlockSpec((LANE, LANE), const)],
        out_specs=[pl.BlockSpec((1, NCOL, LANE), per_b), pl.BlockSpec((1, N_SELECT, LANE), per_b)],
        out_shape=[jax.ShapeDtypeStruct((B, NCOL, LANE), F32), jax.ShapeDtypeStruct((B, N_SELECT, LANE), jnp.int32)],
        scratch_shapes=[pltpu.VMEM((8 + 4 * S_NSB_PAD + 8, LANE), F32)],
        compiler_params=pltpu.CompilerParams(dimension_semantics=("arbitrary",), vmem_limit_bytes=VMEM_LIMIT),
        name="sample_cmp",
    )(q32, kc_s, vc_st, jnp.asarray(info), jnp.asarray(grp))


def _sample_gather_kernel(pt_ref, idx_ref, *refs):
    k_pg, v_pg = refs[0:N_DYN], refs[N_DYN:2 * N_DYN]
    kd_ref, vd_ref, tk, tv = refs[2 * N_DYN:]
    b = pl.program_id(0)
    tg = pl.program_id(1)
    ntg = DEC_SEQ * KV_GROUPS
    for k in range(N_DYN):
        blk = jnp.minimum(idx_ref[b, (k + 3) * ntg + tg], S_NSB - 2)
        r0 = pl.multiple_of((blk % (PAGE_SIZE // SLC_BLOCK)) * SLC_BLOCK, SLC_BLOCK)
        tk[...] = k_pg[k][0, 0].T
        tv[...] = v_pg[k][0, 0].T
        rows = slice(k * SLC_BLOCK, (k + 1) * SLC_BLOCK)
        kd_ref[0, 0, rows, :] = tk[pl.ds(r0, SLC_BLOCK), :].astype(BF16)
        vd_ref[0, 0, rows, :] = tv[pl.ds(r0, SLC_BLOCK), :].astype(BF16)


def sample_gather(l, pool_k, pool_v, page_table, idx):
    B = page_table.shape[0]
    ntg = DEC_SEQ * KV_GROUPS
    per_page = PAGE_SIZE // SLC_BLOCK

    def dyn(k):
        def im(b, tg, pt, ix):
            blk = jnp.minimum(ix[b, (k + 3) * ntg + tg], S_NSB - 2)
            return (l, pt[b, blk // per_page], 0, 0)
        return pl.BlockSpec((1, 1, LANE, PAGE_SIZE), im)
    specs = [dyn(k) for k in range(N_DYN)]
    out = pl.BlockSpec((1, 1, DYN_ROWS, LANE), lambda b, tg, pt, ix: (b, tg, 0, 0))
    gs = pltpu.PrefetchScalarGridSpec(num_scalar_prefetch=2, grid=(B, ntg), in_specs=specs + specs, out_specs=[out, out],
                                      scratch_shapes=[pltpu.VMEM((PAGE_SIZE, LANE), F32), pltpu.VMEM((PAGE_SIZE, LANE), F32)])
    shape = jax.ShapeDtypeStruct((B, ntg, DYN_ROWS, LANE), BF16)
    return pl.pallas_call(
        _sample_gather_kernel, grid_spec=gs, out_shape=[shape, shape],
        compiler_params=pltpu.CompilerParams(dimension_semantics=("arbitrary", "arbitrary"), vmem_limit_bytes=VMEM_LIMIT),
        name="sample_gather",
    )(page_table, idx, *([pool_k] * N_DYN), *([pool_v] * N_DYN))


def _sample_sel_kernel(pt_ref, idx_ref, kd_ref, vd_ref, k0_ref, k1_ref, v0_ref, v1_ref, knew_ref, vnew_ref, kwb_ref, vwb_ref,
                       kwn_ref, vwn_ref, q_ref, gate_ref, oc_ref, info_ref, mask_ref, o_ref, kpos):
    b = pl.program_id(0)
    w = _query_cols(q_ref[0])
    slope = info_ref[0:1, :]
    qpos = info_ref[1:2, :]
    tcol = info_ref[2:3, :]
    okc = info_ref[3:4, :]
    r_f = lax.broadcasted_iota(jnp.int32, (NEW_PAD, LANE), 0).astype(F32)
    cur = (PAST_LEN + DEC_SEQ - 1) // SLC_BLOCK
    ntg = DEC_SEQ * KV_GROUPS
    zpad = jnp.zeros((NEW_PAD - 8, LANE), F32)

    kpos[0:SLC_BLOCK, :] = r_f[0:SLC_BLOCK, :]
    kpos[SLC_BLOCK:ROW_DYN0, :] = float((cur - 1) * SLC_BLOCK) + r_f[0:SLC_BLOCK, :]
    for tg in range(ntg):
        for k in range(N_DYN):
            r0 = ROW_DYN0 + (tg * N_DYN + k) * SLC_BLOCK
            kpos[r0:r0 + SLC_BLOCK, :] = (idx_ref[b, (k + 3) * ntg + tg] * SLC_BLOCK).astype(F32) + r_f[0:SLC_BLOCK, :]
    kpos[ROW_NEW:ROWS_ALL, :] = float(cur * SLC_BLOCK) + r_f

    def gathered(first, prev, dyn_ref, new_ref):
        parts = [first[0, 0].T[0:SLC_BLOCK, :].astype(BF16), prev[0, 0].T[PAGE_SIZE - SLC_BLOCK:, :].astype(BF16)]
        parts += [dyn_ref[0, tg] for tg in range(ntg)]
        return jnp.concatenate(parts + [jnp.concatenate([new_ref[0], zpad], axis=0).astype(BF16)], axis=0)

    def attend(k_rows, v_rows, mask, dist):
        s = jnp.where(mask, _mm(k_rows, w) - slope * dist, NEG)
        p = jnp.exp(s - jnp.max(s, axis=0, keepdims=True))
        l = jnp.sum(p, axis=0, keepdims=True)
        o_t = lax.dot_general(v_rows, p.astype(BF16), (((0,), (0,)), ((), ())), preferred_element_type=F32)
        return (o_t / l).T[0:NCOL, :]

    o_s = attend(gathered(k0_ref, k1_ref, kd_ref, knew_ref), gathered(v0_ref, v1_ref, vd_ref, vnew_ref),
                 mask_ref[...] > 0.5, qpos - kpos[...])
    i_w = lax.broadcasted_iota(jnp.int32, (WIN_ROWS, LANE), 0).astype(F32)
    mask_w = (i_w >= tcol) & (i_w <= tcol + float(WIN_BUF)) & (okc > 0.5)
    kw_rows = jnp.concatenate([kwb_ref[0, 0].T, kwn_ref[0], zpad], axis=0).astype(BF16)
    vw_rows = jnp.concatenate([vwb_ref[0, 0].T, vwn_ref[0], zpad], axis=0).astype(BF16)
    o_w = attend(kw_rows, vw_rows, mask_w, float(WIN_BUF) + tcol - i_w)
    gates = jax.nn.sigmoid(gate_ref[0])
    o_ref[0] = _own_group_lanes(gates[:, 0:1] * oc_ref[0] + gates[:, 1:2] * o_s + gates[:, 2:3] * o_w)


def sample_sel_dense(l, pool_k, pool_v, page_table, idx, knew, vnew, kwin, vwin, kwnew, vwnew, q32, gate32, oc):
    B = page_table.shape[0]
    info, _, _, _ = _col_consts()
    ntg = DEC_SEQ * KV_GROUPS
    per_page = PAGE_SIZE // SLC_BLOCK
    kd, vd = sample_gather(l, pool_k, pool_v, page_table, idx)
    per_b = lambda b, pt, ix: (b, 0, 0)
    const = lambda b, pt, ix: (0, 0)
    dense = pl.BlockSpec((1, ntg, DYN_ROWS, LANE), lambda b, pt, ix: (b, 0, 0, 0))
    first = pl.BlockSpec((1, 1, LANE, PAGE_SIZE), lambda b, pt, ix: (l, pt[b, 0], 0, 0))
    prev = pl.BlockSpec((1, 1, LANE, PAGE_SIZE), lambda b, pt, ix: (l, pt[b, N_PAGES - 1], 0, 0))
    new8 = pl.BlockSpec((1, 8, LANE), per_b)
    win = pl.BlockSpec((1, 1, LANE, WIN_BUF), lambda b, pt, ix: (l, b, 0, 0))
    col32 = pl.BlockSpec((1, NCOL, LANE), per_b)
    gs = pltpu.PrefetchScalarGridSpec(
        num_scalar_prefetch=2, grid=(B,),
        in_specs=[dense, dense, first, prev, first, prev, new8, new8, win, win, new8, new8, col32, col32, col32,
                  pl.BlockSpec((8, LANE), const), pl.BlockSpec((ROWS_ALL, LANE), const)],
        out_specs=col32,
        scratch_shapes=[pltpu.VMEM((ROWS_ALL, LANE), F32)])
    return pl.pallas_call(
        _sample_sel_kernel, grid_spec=gs,
        out_shape=jax.ShapeDtypeStruct((B, NCOL, LANE), F32),
        compiler_params=pltpu.CompilerParams(dimension_semantics=("arbitrary",), vmem_limit_bytes=VMEM_LIMIT),
        name="sample_sel",
    )(page_table, idx, kd, vd, pool_k, pool_k, pool_v, pool_v, knew, vnew, kwin, vwin, kwnew, vwnew,
      q32, gate32, oc, jnp.asarray(info), jnp.asarray(_sel_mask()))


def nsa_sample_attn(l, proj, pools, win, page_table, cw):
    B = page_table.shape[0]
    T = DEC_SEQ
    pool_kc, pool_vc, pool_ks, pool_vs = pools
    n_pool = pool_kc.shape[1]
    pages = lambda p: p.transpose(0, 1, 3, 4, 2).reshape(DEPTH, n_pool, LANE, PAGE_SIZE)
    kc_s, vc_st = compress_paged(l, pages(pool_kc), pages(pool_vc), page_table, cw)
    q32 = proj[:, COL_Q:COL_Q + N_HEADS * LANE].reshape(B, NCOL, LANE)
    oc, idx = sample_cmp(q32, kc_s, vc_st)
    idx = idx[:, :, :T * KV_GROUPS].reshape(B, N_SELECT * T * KV_GROUPS)

    def new8(c):
        return jnp.pad(proj[:, c:c + LANE].reshape(B, T, LANE), ((0, 0), (0, 8 - T), (0, 0)))
    small = proj[:, COL_SMALL:COL_SMALL + LANE].reshape(B, T, LANE)
    gate32 = jnp.pad(small[:, :, SMALL_G:SMALL_G + 3 * N_HEADS].reshape(B, NCOL, 3), ((0, 0), (0, 0), (0, LANE - 3)))
    wins = [w.transpose(0, 1, 3, 4, 2).reshape(DEPTH, B, LANE, WIN_BUF) for w in win]
    o32 = sample_sel_dense(l, pages(pool_ks), pages(pool_vs), page_table, idx, new8(COL_KV + 2 * LANE), new8(COL_KV + 3 * LANE),
                     wins[0], wins[1], new8(COL_KV + 4 * LANE), new8(COL_KV + 5 * LANE), q32, gate32, oc)
    o4 = o32.reshape(B, T, N_HEADS, KV_GROUPS, HEAD_DIM)
    o = jnp.concatenate([o4[:, :, :HEADS_PER_GROUP, 0], o4[:, :, HEADS_PER_GROUP:, 1]], axis=2)
    return o.reshape(B * T, D_B)


def _layer(x, B, T, lw, state, cmp, sample_ctx, g_final, final_norm):
    proj = rms_proj(x, lw["g_mix"], lw["w_in"], lw["b_in"])

    def piece(c0, width):
        return proj[:, c0:c0 + width].reshape(B, T, width)

    m_qk, m_v, m_o = piece(COL_QK, 2 * D_A), piece(COL_V, D_A), piece(COL_O, D_A)
    small = piece(COL_SMALL, LANE)
    kv = [piece(COL_KV + i * KV_W, KV_W).reshape(B, T, KV_GROUPS, HEAD_DIM) for i in range(6)]
    kc, vc, ks, vs, kw, vw = kv

    conv_buf, C, n, m = state
    conv_new = jnp.concatenate([conv_buf, m_qk], axis=1)[:, T:]
    margs = (conv_buf, C, n, m, lw["conv_w"], lw["conv_b"], lw["m_norm"])
    if sample_ctx is None:
        h_a, C1, n1, m1 = mlstm(B, T, proj, COL_QK // (2 * D_A), proj, COL_V // D_A, proj, COL_O // D_A, proj, COL_SMALL // LANE, *margs)
    else:
        def pad_tok(a, fill):
            return jnp.concatenate([a, jnp.broadcast_to(fill, (B, M_CHUNK - T, a.shape[-1]))], axis=1).reshape(B * M_CHUNK, a.shape[-1])
        fill_small = jnp.zeros((LANE,), F32).at[SMALL_I:SMALL_I + M_HEADS].set(NEG).at[SMALL_F:SMALL_F + M_HEADS].set(BIG)
        zero = jnp.zeros((1,), F32)
        h_pad, C1, n1, m1 = mlstm(B, M_CHUNK, pad_tok(m_qk, zero), 0, pad_tok(m_v, zero), 0, pad_tok(m_o, zero), 0, pad_tok(small, fill_small), 0, *margs)
        h_a = h_pad.reshape(B, M_CHUNK, D_A)[:, :T].reshape(B * T, D_A)
    cw = _cmp_weights(*cmp)
    if sample_ctx is None:
        kvc = COL_KV // LANE
        kc_s, vc_st = compress_prompt(B, T, proj, kvc, proj, kvc + 1, cw)
        o_b = nsa_prompt_attn(B, T, proj, kc_s, vc_st)
        keep = min(WINDOW, T)
        kw_state, vw_state = kw[:, T - keep:], vw[:, T - keep:]
    else:
        l, pools, win, page_table = sample_ctx
        o_b = nsa_sample_attn(l, proj, pools, win, page_table, cw)
        keep = min(WINDOW, WIN_BUF + T)
        kw_state = jnp.concatenate([win[0][l], kw], axis=1)[:, WIN_BUF + T - keep:]
        vw_state = jnp.concatenate([win[1][l], vw], axis=1)[:, WIN_BUF + T - keep:]

    x = merge(x, h_a.reshape(B * T, D_A), o_b.reshape(B * T, D_B), proj, lw["w_a"], lw["w_b"], lw["w_o"])
    x = ffn(x, lw["g_ffn"], lw["w_ffn_in"], lw["w_ffn_out"], g_final, final_norm)
    return x, (kc, vc, ks, vs, kw_state, vw_state, conv_new, C1, n1, m1)


def kernel(x_prompt, x_sample, cache_k_cmp, cache_v_cmp, cache_k_slc, cache_v_slc, state_k_win, state_v_win, state_conv, state_C, state_n, state_m, page_table, norm_mix, w_in, b_in, conv_w, conv_b, cmp_pos_k, cmp_pos_v, cmp_proj_k, cmp_proj_v, m_norm, w_a, w_b, w_o, norm_ffn, w_ffn_in, w_ffn_out, norm_final):
    yp = x_prompt.reshape(BATCH * SEQ, D_MODEL)
    ys = x_sample.reshape(DEC_BATCH * DEC_SEQ, D_MODEL)
    p_states, s_states = [], []
    for l in range(DEPTH):
        w_in_l, b_in_l = _prep_w_in(w_in[l], b_in[l])
        lw = dict(g_mix=norm_mix[l], w_in=w_in_l, b_in=b_in_l, conv_w=conv_w[l], conv_b=conv_b[l], m_norm=m_norm[l],
                  w_a=w_a[l].astype(BF16), w_b=w_b[l].astype(BF16), w_o=w_o[l].astype(BF16), g_ffn=norm_ffn[l],
                  w_ffn_in=w_ffn_in[l].astype(BF16), w_ffn_out=w_ffn_out[l].astype(BF16))
        cmp = (cmp_pos_k[l], cmp_pos_v[l], cmp_proj_k[l], cmp_proj_v[l])
        last = l == DEPTH - 1
        conv0 = jnp.zeros((BATCH, CONV_W - 1, 2 * D_A), F32)
        C0 = jnp.zeros((BATCH, M_HEADS, M_HEAD_DIM, M_HEAD_DIM), F32)
        n0 = jnp.zeros((BATCH, M_HEADS, M_HEAD_DIM), F32)
        m0 = jnp.full((BATCH, M_HEADS), NEG, F32)
        yp, st_p = _layer(yp, BATCH, SEQ, lw, (conv0, C0, n0, m0), cmp, None, norm_final, last)
        p_states.append(st_p)
        ctx = (l, (cache_k_cmp, cache_v_cmp, cache_k_slc, cache_v_slc), (state_k_win, state_v_win), page_table)
        ys, st_s = _layer(ys, DEC_BATCH, DEC_SEQ, lw, (state_conv[l], state_C[l], state_n[l], state_m[l]), cmp, ctx, norm_final, last)
        s_states.append(st_s)
    y_prompt = yp.reshape(BATCH, SEQ, D_MODEL)
    y_sample = ys.reshape(DEC_BATCH, DEC_SEQ, D_MODEL)
    P = [jnp.stack(a) for a in zip(*p_states)]
    S = [jnp.stack(a) for a in zip(*s_states)]
    return (y_prompt, y_sample, P[0], S[0], P[1], S[1], P[2], S[2], P[3], S[3], P[4], S[4],
            P[5], S[5], P[6], S[6], P[7], S[7], P[8], S[8], P[9], S[9])
```
